```python
import jax, jax.numpy as jnp
from jax import lax
import numpy as np

D_MODEL = 1024
BATCH = 16
SEQ = 2048
DEPTH = 2
DEC_BATCH = 128
DEC_SEQ = 8
PAST_LEN = 8192
PAGE_SIZE = 128

HEAD_DIM = 64
MLA_HEADS = 6
MLA_Q_LORA = 384
MLA_KV_LORA = 256
MLA_NOPE = 64
MLA_ROPE = 32
MLA_V = HEAD_DIM
MLA_ROW = MLA_KV_LORA + MLA_ROPE
NSA_HEADS = 6
NSA_KV_HEADS = 1
NSA_GROUP = NSA_HEADS // NSA_KV_HEADS
CMP_BLOCK = 32
SLC_BLOCK = 64
RATIO = SLC_BLOCK // CMP_BLOCK
N_SELECT = 8
WINDOW = 512
FORCE_SCORE = 1.0e4
SB_HEADS = 4
MIX_WIDTH = (MLA_HEADS + NSA_HEADS + SB_HEADS) * HEAD_DIM
N_EXPERTS = 64
TOP_K = 8
D_EXPERT = 256
D_SHARED = 256
ROUTED_SCALE = 2.5
ROW_BLOCK = 128
QBLK = 128
ROPE_THETA = 10000.0
LN_EPS = 1e-5
RMS_EPS = 1e-6
ALPHA = (2 * DEPTH) ** 0.25
BETA = (8 * DEPTH) ** -0.25
IN_SPLITS = (MLA_Q_LORA, MLA_ROW, NSA_HEADS * HEAD_DIM, 6 * NSA_KV_HEADS * HEAD_DIM, 3 * NSA_HEADS, 3 * SB_HEADS * HEAD_DIM)

kernel_name = "hybrid_mla_nsa_stickbreak_moe_step"

F32 = jnp.float32


def rms_norm(x, g):
    xf = x.astype(F32)
    y = xf * lax.rsqrt(jnp.mean(xf * xf, -1, keepdims=True) + RMS_EPS)
    return (y * g.astype(F32)).astype(x.dtype)


def layer_norm(x, g, b):
    xf = x.astype(F32)
    mu = jnp.mean(xf, -1, keepdims=True)
    var = jnp.mean(jnp.square(xf - mu), -1, keepdims=True)
    return ((xf - mu) * lax.rsqrt(var + LN_EPS) * g.astype(F32) + b.astype(F32)).astype(x.dtype)


def rope(x, pos):
    half = x.shape[-1] // 2
    inv = 1.0 / (ROPE_THETA ** (jnp.arange(half, dtype=F32) / half))
    ang = pos.astype(F32)[:, None] * inv[None, :]
    cos = jnp.cos(ang)[:, None, :]
    sin = jnp.sin(ang)[:, None, :]
    xf = x.astype(F32)
    x1, x2 = xf[..., :half], xf[..., half:]
    return jnp.concatenate([x1 * cos - x2 * sin, x2 * cos + x1 * sin], -1).astype(x.dtype)


def mixer_inputs(x, pos, w_in, q_norm, w_uq, kv_norm):
    B, T, _ = x.shape
    h = jnp.einsum('btd,dn->btn', x, w_in)
    q_down, kv_down, nq, nkv, ngate, sbqkv = jnp.split(h, np.cumsum(IN_SPLITS)[:-1].tolist(), axis=-1)
    q = jnp.einsum('btr,rhe->bthe', rms_norm(q_down, q_norm), w_uq)
    q_nope = q[..., :MLA_NOPE]
    q_rope = rope(q[..., MLA_NOPE:], pos)
    c_kv = rms_norm(kv_down[..., :MLA_KV_LORA], kv_norm)
    k_rope = rope(kv_down[..., None, MLA_KV_LORA:], pos)[..., 0, :]
    mla_rows = jnp.concatenate([c_kv, k_rope], -1)
    nq_raw = nq.reshape(B, T, NSA_HEADS, HEAD_DIM)
    nq_rot = rope(nq_raw, pos)
    nkv = nkv.reshape(B, T, 3, 2, NSA_KV_HEADS, HEAD_DIM)
    k_rot = rope(nkv[:, :, 1:, 0].reshape(B, T, 2 * NSA_KV_HEADS, HEAD_DIM), pos).reshape(B, T, 2, NSA_KV_HEADS, HEAD_DIM)
    cmp_rows = nkv[:, :, 0]
    slc_rows = jnp.stack([k_rot[:, :, 0], nkv[:, :, 1, 1]], axis=2)
    win_rows = jnp.stack([k_rot[:, :, 1], nkv[:, :, 2, 1]], axis=2)
    gates = jax.nn.sigmoid(ngate.astype(F32)).astype(x.dtype).reshape(B, T, NSA_HEADS, 3)
    sb = sbqkv.reshape(B, T, 3, SB_HEADS, HEAD_DIM)
    return (q_nope, q_rope, mla_rows, nq_raw, nq_rot, cmp_rows, slc_rows, win_rows, gates, sb[:, :, 0], sb[:, :, 1:])


def mla_prompt(q_nope, q_rope, rows, w_uk, w_uv):
    B, T = rows.shape[:2]
    scale = (MLA_NOPE + MLA_ROPE) ** -0.5
    c_kv, k_rope = rows[..., :MLA_KV_LORA], rows[..., MLA_KV_LORA:]
    k_nope = jnp.einsum('btr,rhn->bthn', c_kv, w_uk)
    v = jnp.einsum('btr,rhv->bthv', c_kv, w_uv)
    nblk = T // QBLK
    qn = q_nope.reshape(B, nblk, QBLK, MLA_HEADS, MLA_NOPE).swapaxes(0, 1)
    qr = q_rope.reshape(B, nblk, QBLK, MLA_HEADS, MLA_ROPE).swapaxes(0, 1)
    kpos = jnp.arange(T)

    def blk(args):
        i, qn_b, qr_b = args
        qpos = i * QBLK + jnp.arange(QBLK)
        s = (jnp.einsum('bqhn,bkhn->bhqk', qn_b, k_nope) + jnp.einsum('bqhe,bke->bhqk', qr_b, k_rope)).astype(F32) * scale
        s = jnp.where(kpos[None, :] <= qpos[:, None], s, -jnp.inf)
        p = jax.nn.softmax(s, -1).astype(v.dtype)
        return jnp.einsum('bhqk,bkhv->bqhv', p, v)

    o = lax.map(blk, (jnp.arange(nblk), qn, qr))
    return o.swapaxes(0, 1).reshape(B, T, MLA_HEADS * MLA_V)


def mla_sample(q_nope, q_rope, new_rows, past_rows, w_uk, w_uv):
    B, S = new_rows.shape[:2]
    P = past_rows.shape[1]
    scale = (MLA_NOPE + MLA_ROPE) ** -0.5
    q_cat = jnp.concatenate([jnp.einsum('bqhn,rhn->bqhr', q_nope, w_uk), q_rope], -1)
    s_past = jnp.einsum('bqhc,bkc->bhqk', q_cat, past_rows).astype(F32) * scale
    s_new = jnp.einsum('bqhc,bkc->bhqk', q_cat, new_rows).astype(F32) * scale
    causal = jnp.arange(S)[None, :] <= jnp.arange(S)[:, None]
    s_new = jnp.where(causal, s_new, -jnp.inf)
    p = jax.nn.softmax(jnp.concatenate([s_past, s_new], -1), -1).astype(past_rows.dtype)
    o_lat = jnp.einsum('bhqk,bkc->bqhc', p[..., :P], past_rows) + jnp.einsum('bhqk,bkc->bqhc', p[..., P:], new_rows)
    o = jnp.einsum('bqhr,rhv->bqhv', o_lat[..., :MLA_KV_LORA], w_uv)
    return o.reshape(B, S, MLA_HEADS * MLA_V)


def nsa_compress(rows, pe, w1, w2):
    B, T = rows.shape[:2]
    nb = T // CMP_BLOCK
    blk = rows.reshape(B, nb, CMP_BLOCK, 2, NSA_KV_HEADS, HEAD_DIM) + pe
    blk = blk.transpose(0, 1, 3, 4, 2, 5).reshape(B, nb, 2, NSA_KV_HEADS, CMP_BLOCK * HEAD_DIM)
    h = jax.nn.gelu(jnp.einsum('bnckf,ckfe->bncke', blk, w1))
    return jnp.einsum('bncke,ckef->bnckf', h, w2)


def nsa_cmp_branch(q, qpos, cblk):
    B, Tq = q.shape[:2]
    NB = cblk.shape[1]
    qg = q.reshape(B, Tq, NSA_KV_HEADS, NSA_GROUP, HEAD_DIM)
    s = jnp.einsum('bqkgd,bnkd->bqkgn', qg, cblk[:, :, 0]).astype(F32) * HEAD_DIM ** -0.5
    valid = ((jnp.arange(NB) + 1) * CMP_BLOCK - 1)[None, :] <= qpos[:, None]
    s = jnp.where(valid[None, :, None, None, :], s, -jnp.inf)
    m = jnp.max(s, -1, keepdims=True)
    e = jnp.exp(s - jnp.where(jnp.isfinite(m), m, 0.0))
    p = e / jnp.maximum(e.sum(-1, keepdims=True), 1.0)
    o = jnp.einsum('bqkgn,bnkd->bqkgd', p.astype(cblk.dtype), cblk[:, :, 1]).reshape(B, Tq, NSA_HEADS, HEAD_DIM)
    imp = p.sum(3).reshape(B, Tq, NSA_KV_HEADS, NB // RATIO, RATIO).sum(-1)
    return o, imp


def nsa_select(imp, qpos):
    NS = imp.shape[-1]
    blk = jnp.arange(NS)
    cur = qpos // SLC_BLOCK
    forced = (blk[None, :] == 0) | (blk[None, :] == cur[:, None])
    future = blk[None, :] > cur[:, None]
    score = jnp.where(forced[:, None, :], FORCE_SCORE, jnp.where(future[:, None, :], -1.0, imp))
    _, idx = lax.top_k(score, min(N_SELECT, NS))
    return idx


def nsa_slc_attend(q_rot, qpos, sel_kv, idx):
    B, Tq = q_rot.shape[:2]
    qg = q_rot.reshape(B, Tq, NSA_KV_HEADS, NSA_GROUP, HEAD_DIM)
    s = jnp.einsum('bqkgd,bqknsd->bqkgns', qg, sel_kv[..., 0, :]).astype(F32) * HEAD_DIM ** -0.5
    kpos = idx[..., None] * SLC_BLOCK + jnp.arange(SLC_BLOCK)
    ok = kpos <= qpos[None, :, None, None, None]
    s = jnp.where(ok[:, :, :, None], s, -jnp.inf)
    p = jax.nn.softmax(s, axis=(-2, -1)).astype(sel_kv.dtype)
    o = jnp.einsum('bqkgns,bqknsd->bqkgd', p, sel_kv[..., 1, :])
    return o.reshape(B, Tq, NSA_HEADS, HEAD_DIM)


def nsa_win_attend(q_rot, qpos, kv, kpos):
    B, Tq = q_rot.shape[:2]
    qg = q_rot.reshape(B, Tq, NSA_KV_HEADS, NSA_GROUP, HEAD_DIM)
    s = jnp.einsum('bqkgd,btkd->bqkgt', qg, kv[:, :, 0]).astype(F32) * HEAD_DIM ** -0.5
    ok = (kpos[None, :] <= qpos[:, None]) & (kpos[None, :] > qpos[:, None] - WINDOW) & (kpos[None, :] >= 0)
    s = jnp.where(ok[None, :, None, None, :], s, -jnp.inf)
    p = jax.nn.softmax(s, -1).astype(kv.dtype)
    o = jnp.einsum('bqkgt,btkd->bqkgd', p, kv[:, :, 1])
    return o.reshape(B, Tq, NSA_HEADS, HEAD_DIM)


def nsa_gate(gates, o_cmp, o_slc, o_win):
    B, T = gates.shape[:2]
    o = gates[..., 0, None] * o_cmp + gates[..., 1, None] * o_slc + gates[..., 2, None] * o_win
    return o.reshape(B, T, NSA_HEADS * HEAD_DIM)


def nsa_prompt(q_raw, q_rot, cmp_rows, slc_rows, win_rows, gates, pe, w1, w2):
    B, T = q_raw.shape[:2]
    pos = jnp.arange(T)
    o_cmp, imp = nsa_cmp_branch(q_raw, pos, nsa_compress(cmp_rows, pe, w1, w2))
    idx = nsa_select(imp, pos)
    n = idx.shape[-1]
    NS = T // SLC_BLOCK
    slc_blocks = slc_rows.reshape(B, NS, SLC_BLOCK, 2, NSA_KV_HEADS, HEAD_DIM).transpose(0, 4, 1, 2, 3, 5)
    win_pad = jnp.pad(win_rows, ((0, 0), (WINDOW, 0), (0, 0), (0, 0), (0, 0)))
    bi = jnp.arange(B)[:, None, None, None]
    hi = jnp.arange(NSA_KV_HEADS)[None, None, :, None]
    nblk = T // QBLK

    def blk(args):
        i, qb, ib = args
        qpos = i * QBLK + jnp.arange(QBLK)
        sel = slc_blocks[bi, hi, ib]
        o_s = nsa_slc_attend(qb, qpos, sel, ib)
        kv_w = lax.dynamic_slice_in_dim(win_pad, i * QBLK, WINDOW + QBLK, axis=1)
        kpos = i * QBLK - WINDOW + jnp.arange(WINDOW + QBLK)
        o_w = nsa_win_attend(qb, qpos, kv_w, kpos)
        return o_s, o_w

    qb_all = q_rot.reshape(B, nblk, QBLK, NSA_HEADS, HEAD_DIM).swapaxes(0, 1)
    ib_all = idx.reshape(B, nblk, QBLK, NSA_KV_HEADS, n).swapaxes(0, 1)
    o_s, o_w = lax.map(blk, (jnp.arange(nblk), qb_all, ib_all))
    o_s = o_s.swapaxes(0, 1).reshape(B, T, NSA_HEADS, HEAD_DIM)
    o_w = o_w.swapaxes(0, 1).reshape(B, T, NSA_HEADS, HEAD_DIM)
    return nsa_gate(gates, o_cmp, o_s, o_w)


def nsa_sample(q_raw, q_rot, cmp_rows, slc_rows, win_rows, gates, past_cmp, cache_slc, layer, page_table, win_buf, pe, w1, w2):
    DB, S = q_raw.shape[:2]
    P = past_cmp.shape[1]
    pos = P + jnp.arange(S)
    Sp = -(-S // SLC_BLOCK) * SLC_BLOCK
    pad = ((0, 0), (0, Sp - S), (0, 0), (0, 0), (0, 0))
    cblk = jnp.concatenate([nsa_compress(past_cmp, pe, w1, w2), nsa_compress(jnp.pad(cmp_rows, pad), pe, w1, w2)], axis=1)
    o_cmp, imp = nsa_cmp_branch(q_raw, pos, cblk)
    idx = nsa_select(imp, pos)
    NSP = P // SLC_BLOCK
    NSN = Sp // SLC_BLOCK
    PPB = PAGE_SIZE // SLC_BLOCK
    bi = jnp.arange(DB)[:, None, None, None]
    hi = jnp.arange(NSA_KV_HEADS)[None, None, :, None]
    b_past = jnp.minimum(idx, NSP - 1)
    phys = page_table[bi, b_past // PPB]
    rows_idx = (b_past % PPB)[..., None] * SLC_BLOCK + jnp.arange(SLC_BLOCK)
    past_sel = cache_slc[layer, phys[..., None], rows_idx, :, hi[..., None], :]
    new_blocks = jnp.pad(slc_rows, pad).reshape(DB, NSN, SLC_BLOCK, 2, NSA_KV_HEADS, HEAD_DIM).transpose(0, 4, 1, 2, 3, 5)
    new_sel = new_blocks[bi, hi, jnp.clip(idx - NSP, 0, NSN - 1)]
    sel = jnp.where((idx >= NSP)[..., None, None, None], new_sel, past_sel)
    o_s = nsa_slc_attend(q_rot, pos, sel, idx)
    Wr = win_buf.shape[1]
    kv_w = jnp.concatenate([win_buf, win_rows], axis=1)
    o_w = nsa_win_attend(q_rot, pos, kv_w, P - Wr + jnp.arange(Wr + S))
    new_win = kv_w[:, kv_w.shape[1] - min(WINDOW, Wr + S):]
    return nsa_gate(gates, o_cmp, o_s, o_w), new_win


def sb_weights(z, causal):
    log_1mb = jnp.where(causal, jax.nn.log_sigmoid(-z), 0.0)
    suffix = lax.cumsum(log_1mb, axis=z.ndim - 1, reverse=True) - log_1mb
    return jnp.where(causal, jnp.exp(jax.nn.log_sigmoid(z) + suffix), 0.0)


def sb_prompt(q, rows):
    B, T = q.shape[:2]
    k, v = rows[:, :, 0], rows[:, :, 1]
    kpos = jnp.arange(T)
    nblk = T // QBLK

    def blk(args):
        i, qb = args
        qpos = i * QBLK + jnp.arange(QBLK)
        z = jnp.einsum('bqhd,bkhd->bhqk', qb, k).astype(F32) * HEAD_DIM ** -0.5
        a = sb_weights(z, kpos[None, :] < qpos[:, None]).astype(v.dtype)
        return jnp.einsum('bhqk,bkhd->bqhd', a, v)

    o = lax.map(blk, (jnp.arange(nblk), q.reshape(B, nblk, QBLK, SB_HEADS, HEAD_DIM).swapaxes(0, 1)))
    return o.swapaxes(0, 1).reshape(B, T, SB_HEADS * HEAD_DIM)


def sb_sample(q, new_rows, past_k, past_v):
    B, S = q.shape[:2]
    P = past_k.shape[1]
    z = jnp.concatenate([jnp.einsum('bqhd,bkhd->bhqk', q, past_k),
                         jnp.einsum('bqhd,bkhd->bhqk', q, new_rows[:, :, 0])], -1).astype(F32) * HEAD_DIM ** -0.5
    kpos = jnp.arange(P + S)
    qpos = P + jnp.arange(S)
    a = sb_weights(z, kpos[None, :] < qpos[:, None]).astype(q.dtype)
    o = jnp.einsum('bhqk,bkhd->bqhd', a[..., :P], past_v) + jnp.einsum('bhqk,bkhd->bqhd', a[..., P:], new_rows[:, :, 1])
    return o.reshape(B, S, SB_HEADS * HEAD_DIM)


def mix_out(o_mla, o_nsa, o_sb, g, w_out):
    a = MLA_HEADS * MLA_V
    b = a + NSA_HEADS * HEAD_DIM
    o = jnp.concatenate([rms_norm(o_mla, g[:a]), rms_norm(o_nsa, g[a:b]), rms_norm(o_sb, g[b:])], -1)
    return jnp.einsum('btm,md->btd', o, w_out)


def swiglu(x, wg, wu, wd):
    return (jax.nn.silu(x @ wg) * (x @ wu)) @ wd


def moe_routed(xt, idx, gw, w_gate, w_up, w_down):
    T, D = xt.shape
    M = T * TOP_K
    e_flat = idx.reshape(M)
    order = jnp.argsort(e_flat)
    e_sorted = e_flat[order]
    tok_sorted = order // TOP_K
    g_sorted = gw.reshape(M)[order]
    counts = jnp.bincount(e_flat, length=N_EXPERTS)
    padded = (counts + ROW_BLOCK - 1) // ROW_BLOCK * ROW_BLOCK
    pad_end = jnp.cumsum(padded)
    pad_start = pad_end - padded
    grp_start = jnp.cumsum(counts) - counts
    dest = pad_start[e_sorted] + jnp.arange(M) - grp_start[e_sorted]
    n_blk = (M + N_EXPERTS * (ROW_BLOCK - 1) + ROW_BLOCK - 1) // ROW_BLOCK
    xbuf = jnp.zeros((n_blk * ROW_BLOCK, D), xt.dtype).at[dest].set(xt[tok_sorted])
    blk_expert = jnp.minimum(jnp.searchsorted(pad_end, jnp.arange(n_blk) * ROW_BLOCK, side='right'), N_EXPERTS - 1)

    def expert_block(args):
        xb, e = args
        return swiglu(xb, w_gate[e], w_up[e], w_down[e])

    ybuf = lax.map(expert_block, (xbuf.reshape(n_blk, ROW_BLOCK, D), blk_expert)).reshape(n_blk * ROW_BLOCK, D)
    return jax.ops.segment_sum(ybuf[dest] * g_sorted[:, None], tok_sorted, num_segments=T)


def moe(x, w_router, router_bias, w_gate, w_up, w_down, sh_gate, sh_up, sh_down):
    B, T, D = x.shape
    xt = x.reshape(B * T, D)
    s = jax.nn.sigmoid(jnp.einsum('td,de->te', xt, w_router).astype(F32))
    _, idx = lax.top_k(s + router_bias.astype(F32), TOP_K)
    sel = jnp.take_along_axis(s, idx, -1)
    gw = (sel / sel.sum(-1, keepdims=True) * ROUTED_SCALE).astype(x.dtype)
    y = moe_routed(xt, idx, gw, w_gate, w_up, w_down) + swiglu(xt, sh_gate, sh_up, sh_down)
    return y.reshape(B, T, D)


def setup_inputs(seed: int = 0) -> dict:
    key = jax.random.key(seed)
    ks = iter(jax.random.split(key, 40))

    def nrm(shape, scale):
        return jax.random.normal(next(ks), shape, F32) * scale

    n_pages = PAST_LEN // PAGE_SIZE
    n_phys = (DEC_BATCH * n_pages * 5) // 4
    win_rows = min(WINDOW, PAST_LEN)
    n_in = int(sum(IN_SPLITS))
    page_table = jax.random.permutation(next(ks), n_phys)[:DEC_BATCH * n_pages].reshape(DEC_BATCH, n_pages).astype(jnp.int32)
    return {
        "x_prompt": nrm((BATCH, SEQ, D_MODEL), 1.0),
        "x_sample": nrm((DEC_BATCH, DEC_SEQ, D_MODEL), 1.0),
        "cache_mla": nrm((DEPTH, n_phys, PAGE_SIZE, MLA_ROW), 1.0),
        "cache_nsa_cmp": nrm((DEPTH, n_phys, PAGE_SIZE, 2, NSA_KV_HEADS, HEAD_DIM), 1.0),
        "cache_nsa_slc": nrm((DEPTH, n_phys, PAGE_SIZE, 2, NSA_KV_HEADS, HEAD_DIM), 1.0),
        "state_nsa_win": nrm((DEPTH, DEC_BATCH, win_rows, 2, NSA_KV_HEADS, HEAD_DIM), 1.0),
        "cache_sb": nrm((DEPTH, n_phys, PAGE_SIZE, 2, SB_HEADS, HEAD_DIM), 1.0),
        "page_table": page_table,
        "w_in": nrm((DEPTH, D_MODEL, n_in), D_MODEL ** -0.5),
        "mla_q_norm": 1.0 + nrm((DEPTH, MLA_Q_LORA), 0.02),
        "mla_w_uq": nrm((DEPTH, MLA_Q_LORA, MLA_HEADS, MLA_NOPE + MLA_ROPE), MLA_Q_LORA ** -0.5),
        "mla_kv_norm": 1.0 + nrm((DEPTH, MLA_KV_LORA), 0.02),
        "mla_w_uk": nrm((DEPTH, MLA_KV_LORA, MLA_HEADS, MLA_NOPE), MLA_KV_LORA ** -0.5),
        "mla_w_uv": nrm((DEPTH, MLA_KV_LORA, MLA_HEADS, MLA_V), MLA_KV_LORA ** -0.5),
        "nsa_cmp_pe": nrm((DEPTH, CMP_BLOCK, 2, NSA_KV_HEADS, HEAD_DIM), 0.5),
        "nsa_cmp_w1": nrm((DEPTH, 2, NSA_KV_HEADS, CMP_BLOCK * HEAD_DIM, HEAD_DIM), (CMP_BLOCK * HEAD_DIM) ** -0.5),
        "nsa_cmp_w2": nrm((DEPTH, 2, NSA_KV_HEADS, HEAD_DIM, HEAD_DIM), HEAD_DIM ** -0.5),
        "grp_norm": 1.0 + nrm((DEPTH, MIX_WIDTH), 0.02),
        "w_out": nrm((DEPTH, MIX_WIDTH, D_MODEL), MIX_WIDTH ** -0.5 * BETA),
        "ln1_g": 1.0 + nrm((DEPTH, D_MODEL), 0.02),
        "ln1_b": nrm((DEPTH, D_MODEL), 0.02),
        "w_router": nrm((DEPTH, D_MODEL, N_EXPERTS), D_MODEL ** -0.5),
        "router_bias": nrm((DEPTH, N_EXPERTS), 0.01),
        "moe_w_gate": nrm((DEPTH, N_EXPERTS, D_MODEL, D_EXPERT), D_MODEL ** -0.5),
        "moe_w_up": nrm((DEPTH, N_EXPERTS, D_MODEL, D_EXPERT), D_MODEL ** -0.5),
        "moe_w_down": nrm((DEPTH, N_EXPERTS, D_EXPERT, D_MODEL), D_EXPERT ** -0.5 * BETA),
        "sh_w_gate": nrm((DEPTH, D_MODEL, D_SHARED), D_MODEL ** -0.5),
        "sh_w_up": nrm((DEPTH, D_MODEL, D_SHARED), D_MODEL ** -0.5),
        "sh_w_down": nrm((DEPTH, D_SHARED, D_MODEL), D_SHARED ** -0.5 * BETA),
        "ln2_g": 1.0 + nrm((DEPTH, D_MODEL), 0.02),
        "ln2_b": nrm((DEPTH, D_MODEL), 0.02),
    }


def reference(x_prompt, x_sample, cache_mla, cache_nsa_cmp, cache_nsa_slc, state_nsa_win, cache_sb, page_table,
              w_in, mla_q_norm, mla_w_uq, mla_kv_norm, mla_w_uk, mla_w_uv, nsa_cmp_pe, nsa_cmp_w1, nsa_cmp_w2,
              grp_norm, w_out, ln1_g, ln1_b, w_router, router_bias, moe_w_gate, moe_w_up, moe_w_down,
              sh_w_gate, sh_w_up, sh_w_down, ln2_g, ln2_b):
    T = x_prompt.shape[1]
    DB, S, _ = x_sample.shape
    P = page_table.shape[1] * PAGE_SIZE
    pos_p = jnp.arange(T)
    pos_s = P + jnp.arange(S)
    hp, hs = x_prompt, x_sample
    mla_p, mla_s, cmp_p, cmp_s, slc_p, slc_s, win_p, win_s, sb_p, sb_s = ([] for _ in range(10))
    for l in range(DEPTH):
        moe_w = (w_router[l], router_bias[l], moe_w_gate[l], moe_w_up[l], moe_w_down[l], sh_w_gate[l], sh_w_up[l], sh_w_down[l])
        qn, qr, mla_r, nq_raw, nq_rot, cmp_r, slc_r, win_r, gts, sbq, sb_r = mixer_inputs(
            hp, pos_p, w_in[l], mla_q_norm[l], mla_w_uq[l], mla_kv_norm[l])
        o_mla = mla_prompt(qn, qr, mla_r, mla_w_uk[l], mla_w_uv[l])
        o_nsa = nsa_prompt(nq_raw, nq_rot, cmp_r, slc_r, win_r, gts, nsa_cmp_pe[l], nsa_cmp_w1[l], nsa_cmp_w2[l])
        o_sb = sb_prompt(sbq, sb_r)
        hp = layer_norm(ALPHA * hp + mix_out(o_mla, o_nsa, o_sb, grp_norm[l], w_out[l]), ln1_g[l], ln1_b[l])
        hp = layer_norm(ALPHA * hp + moe(hp, *moe_w), ln2_g[l], ln2_b[l])
        mla_p.append(mla_r)
        cmp_p.append(cmp_r)
        slc_p.append(slc_r)
        win_p.append(win_r[:, T - min(WINDOW, T):])
        sb_p.append(sb_r)
        qn, qr, mla_r, nq_raw, nq_rot, cmp_r, slc_r, win_r, gts, sbq, sb_r = mixer_inputs(
            hs, pos_s, w_in[l], mla_q_norm[l], mla_w_uq[l], mla_kv_norm[l])
        past_mla = cache_mla[l, page_table].reshape(DB, P, MLA_ROW)
        past_cmp = cache_nsa_cmp[l, page_table].reshape(DB, P, 2, NSA_KV_HEADS, HEAD_DIM)
        past_sb_k = cache_sb[l, page_table, :, 0].reshape(DB, P, SB_HEADS, HEAD_DIM)
        past_sb_v = cache_sb[l, page_table, :, 1].reshape(DB, P, SB_HEADS, HEAD_DIM)
        o_mla = mla_sample(qn, qr, mla_r, past_mla, mla_w_uk[l], mla_w_uv[l])
        o_nsa, new_win = nsa_sample(nq_raw, nq_rot, cmp_r, slc_r, win_r, gts, past_cmp, cache_nsa_slc, l, page_table,
                                    state_nsa_win[l], nsa_cmp_pe[l], nsa_cmp_w1[l], nsa_cmp_w2[l])
        o_sb = sb_sample(sbq, sb_r, past_sb_k, past_sb_v)
        hs = layer_norm(ALPHA * hs + mix_out(o_mla, o_nsa, o_sb, grp_norm[l], w_out[l]), ln1_g[l], ln1_b[l])
        hs = layer_norm(ALPHA * hs + moe(hs, *moe_w), ln2_g[l], ln2_b[l])
        mla_s.append(mla_r)
        cmp_s.append(cmp_r)
        slc_s.append(slc_r)
        win_s.append(new_win)
        sb_s.append(sb_r)
    return (hp, hs, jnp.stack(mla_p), jnp.stack(mla_s), jnp.stack(cmp_p), jnp.stack(cmp_s), jnp.stack(slc_p), jnp.stack(slc_s), jnp.stack(win_p), jnp.stack(win_s), jnp.stack(sb_p), jnp.stack(sb_s))
```

```python
import functools

import jax
import jax.numpy as jnp
from jax import lax
from jax.experimental import pallas as pl
from jax.experimental.pallas import tpu as pltpu

F32 = jnp.float32
BF = jnp.bfloat16
I32 = jnp.int32

PAGE = 128
HD = 64
MLA_H = 6
MLA_QL = 384
MLA_KVL = 256
MLA_NOPE = 64
MLA_ROPE = 32
MLA_ROW = MLA_KVL + MLA_ROPE
NSA_H = 6
CMP_BLOCK = 32
SLC_BLOCK = 64
RATIO = SLC_BLOCK // CMP_BLOCK
N_SELECT = 8
WINDOW = 512
FORCE_SCORE = 1.0e4
SB_H = 4
N_EXP = 64
TOP_K = 8
D_EXP = 256
ROUTED_SCALE = 2.5
ROPE_THETA = 10000.0
LN_EPS = 1e-5
RMS_EPS = 1e-6
DEPTH = 2
ALPHA = (2 * DEPTH) ** 0.25
IN_SPLITS = (MLA_QL, MLA_ROW, NSA_H * HD, 6 * HD, 3 * NSA_H, 3 * SB_H * HD)
W_EXT = 3200
NEG = -1e30

LANES = 128
VMEM_LIMIT = 56 * 1024 * 1024
MOE_ROWS = 256


def _cp(sem):
    return pltpu.CompilerParams(dimension_semantics=sem, vmem_limit_bytes=VMEM_LIMIT)


def _bdot(a, b):
    return jnp.dot(a.astype(BF), b.astype(BF), preferred_element_type=F32)


def _bdot_nt(a, b):
    return lax.dot_general(a.astype(BF), b.astype(BF), (((1,), (1,)), ((), ())), preferred_element_type=F32)


def _split3(x):
    hi = x.astype(BF)
    r1 = x - hi.astype(F32)
    mid = r1.astype(BF)
    lo = (r1 - mid.astype(F32)).astype(BF)
    return hi, mid, lo


def _dot3(x, m01):
    hi, mid, lo = _split3(x)
    return (jnp.dot(hi, m01, preferred_element_type=F32) + jnp.dot(mid, m01, preferred_element_type=F32)
            + jnp.dot(lo, m01, preferred_element_type=F32))


def _lane(shape):
    return lax.broadcasted_iota(I32, shape, len(shape) - 1)


def _tile_l(x, n):
    return jnp.concatenate([x] * n, axis=1)


def _tile_r(x, n):
    return jnp.concatenate([x] * n, axis=0)


def _tile_mask(ok, n):
    return _tile_r(ok.astype(I32), n) > 0


def _round_up(n, m):
    return -(-n // m) * m


def _proj_kernel(x_ref, w_ref, qn_ref, wuq_ref, kvn_ref, wuk_ref, wuv_ref, tab_ref,
                 rows_ref, q_ref, k_ref, v_ref, nqr_ref, nqt_ref, cmp_ref, slc_ref, win_ref,
                 slcb_ref, winb_ref, g_ref, sbq_ref, sbr_ref, sbb_ref):
    h = _bdot(x_ref[...], w_ref[...])
    tab = tab_ref[...]
    cq, sq, ckr, skr, cn, sn, c2, s2 = [tab[:, i * LANES:(i + 1) * LANES] for i in range(8)]
    qd = h[:, 0:384]
    qd = qd * lax.rsqrt(jnp.mean(qd * qd, -1, keepdims=True) + RMS_EPS) * qn_ref[...]
    ql = _bdot(qd, wuq_ref[...])
    q_ref[...] = (ql[:, :768] * _tile_l(cq, 6) + ql[:, 768:] * _tile_l(sq, 6)).astype(BF)
    ckv = h[:, 384:640]
    ckv = ckv * lax.rsqrt(jnp.mean(ckv * ckv, -1, keepdims=True) + RMS_EPS) * kvn_ref[...]
    kr = h[:, 640:768] * ckr + h[:, 768:896] * skr
    rows_ref[:, 0:MLA_KVL] = ckv
    rows_ref[:, MLA_KVL:MLA_ROW] = kr[:, 0:MLA_ROPE]
    kr_hi = jnp.where(_lane(kr.shape) >= 64, kr, 0.0)
    k_ref[...] = (_bdot(ckv, wuk_ref[...]) + _tile_l(kr_hi, 6)).astype(BF)
    v_ref[...] = _bdot(ckv, wuv_ref[...]).astype(BF)
    nq = h[:, 896:1280]
    nqr_ref[...] = nq.astype(BF)
    nqt_ref[...] = (nq * _tile_l(cn, 3) + h[:, 1280:1664] * _tile_l(sn, 3)).astype(BF)
    cmp_ref[...] = h[:, 1664:1792]
    slc = h[:, 1792:1920] * c2 + h[:, 2048:2176] * s2
    win = h[:, 1920:2048] * c2 + h[:, 2176:2304] * s2
    slc_ref[...] = slc
    win_ref[...] = win
    slcb_ref[...] = slc.astype(BF)
    winb_ref[...] = win.astype(BF)
    g_ref[...] = jax.nn.sigmoid(h[:, 2304:2432])
    sbq_ref[...] = h[:, 2432:2688].astype(BF)
    sbr = h[:, 2688:3200]
    sbr_ref[...] = sbr
    sbb_ref[...] = sbr.astype(BF)


def _proj(x, wl, tab, tm, n_tab_blocks):
    n, d = x.shape
    outs = [
        ((n, MLA_ROW), F32), ((n, 768), BF), ((n, 768), BF), ((n, 384), BF),
        ((n, 384), BF), ((n, 384), BF), ((n, 128), F32), ((n, 128), F32), ((n, 128), F32),
        ((n, 128), BF), ((n, 128), BF), ((n, 128), F32), ((n, 256), BF), ((n, 512), F32), ((n, 512), BF),
    ]
    full = lambda a: pl.BlockSpec(a.shape, lambda i: (0,) * a.ndim)
    return pl.pallas_call(
        _proj_kernel,
        grid=(n // tm,),
        in_specs=[pl.BlockSpec((tm, d), lambda i: (i, 0)), full(wl["w_ext"]), full(wl["q_norm"]), full(wl["wuq"]),
                  full(wl["kv_norm"]), full(wl["wuk"]), full(wl["wuv"]),
                  pl.BlockSpec((tm, 1024), lambda i: (i % n_tab_blocks, 0))],
        out_specs=[pl.BlockSpec((tm, s[1]), lambda i: (i, 0)) for s, _ in outs],
        out_shape=[jax.ShapeDtypeStruct(s, dt) for s, dt in outs],
        compiler_params=_cp(("parallel",)),
        name="proj",
    )(x, wl["w_ext"], wl["q_norm"], wl["wuq"], wl["kv_norm"], wl["wuk"], wl["wuv"], tab)


def _softmax_step(s, carry, v):
    m, l, acc = carry
    m_new = jnp.maximum(m, jnp.max(s, axis=1, keepdims=True))
    alpha = jnp.exp(m - m_new)
    p = jnp.exp(s - m_new)
    l = alpha * l + jnp.sum(p, axis=1, keepdims=True)
    acc = alpha * acc + jnp.dot(p.astype(BF), v, preferred_element_type=F32)
    return m_new, l, acc


def _mla_attn_kernel(q_ref, k_ref, v_ref, o_ref, *, tq):
    i = pl.program_id(1)
    scale = (MLA_NOPE + MLA_ROPE) ** -0.5
    lo = _lane((tq, LANES)) < 64
    diag = lax.broadcasted_iota(I32, (tq, tq), 1) <= lax.broadcasted_iota(I32, (tq, tq), 0)
    for j in range(MLA_H // 2):
        o_pair = None
        for hh in range(2):
            h = 2 * j + hh
            q = q_ref[:, h * LANES:(h + 1) * LANES]
            keep = lo if hh == 0 else jnp.logical_not(lo)

            def step(kb, carry, masked, q=q, h=h, keep=keep, j=j):
                ks = pl.multiple_of(kb * tq, tq)
                k = k_ref[pl.ds(ks, tq), h * LANES:(h + 1) * LANES]
                s = _bdot_nt(q, k) * scale
                if masked:
                    s = jnp.where(diag, s, NEG)
                v = v_ref[pl.ds(ks, tq), j * LANES:(j + 1) * LANES]
                v = jnp.where(keep, v, jnp.zeros_like(v))
                return _softmax_step(s, carry, v)

            init = (jnp.full((tq, 1), NEG, F32), jnp.zeros((tq, 1), F32), jnp.zeros((tq, LANES), F32))
            carry = lax.fori_loop(0, i, functools.partial(step, masked=False), init)
            _, l, acc = step(i, carry, True)
            o_h = acc / l
            o_pair = o_h if o_pair is None else o_pair + o_h
        o_ref[:, j * LANES:(j + 1) * LANES] = o_pair


def _mla_attn(q, k, v, b, t, tq):
    return pl.pallas_call(
        functools.partial(_mla_attn_kernel, tq=tq),
        grid=(b, t // tq),
        in_specs=[pl.BlockSpec((tq, 768), lambda bi, i: (bi * (t // tq) + i, 0)),
                  pl.BlockSpec((t, 768), lambda bi, i: (bi, 0)),
                  pl.BlockSpec((t, 384), lambda bi, i: (bi, 0))],
        out_specs=pl.BlockSpec((tq, 384), lambda bi, i: (bi * (t // tq) + i, 0)),
        out_shape=jax.ShapeDtypeStruct((b * t, 384), F32),
        compiler_params=_cp(("parallel", "parallel")),
        name="mla_attn",
    )(q, k, v)


def _sb_terms(z):
    lg = jnp.log1p(jnp.exp(-jnp.abs(z)))
    return -(jnp.maximum(z, 0.0) + lg), -(jnp.maximum(-z, 0.0) + lg)


def _sb_attn_kernel(q_ref, kv_ref, o_ref, *, tq):
    i = pl.program_id(1)
    scale = HD ** -0.5
    lo = _lane((tq, LANES)) < 64
    r = lax.broadcasted_iota(I32, (tq, tq), 0)
    c = lax.broadcasted_iota(I32, (tq, tq), 1)
    strict = c < r
    upper = (r > c).astype(BF)
    for j in range(SB_H // 2):
        qp = q_ref[:, j * LANES:(j + 1) * LANES]
        o_pair = None
        for hh in range(2):
            keep = lo if hh == 0 else jnp.logical_not(lo)
            q = jnp.where(keep, qp, jnp.zeros_like(qp))

            def step(n, carry, masked, q=q, keep=keep, j=j):
                run, acc = carry
                kb = i - n
                ks = pl.multiple_of(kb * tq, tq)
                k = kv_ref[pl.ds(ks, tq), j * LANES:(j + 1) * LANES]
                z = _bdot_nt(q, k) * scale
                l1mb, lsz = _sb_terms(z)
                if masked:
                    l1mb = jnp.where(strict, l1mb, 0.0)
                suffix = _dot3(l1mb, upper) + run
                a = jnp.exp(lsz + suffix)
                if masked:
                    a = jnp.where(strict, a, 0.0)
                v = kv_ref[pl.ds(ks, tq), 256 + j * LANES:256 + (j + 1) * LANES]
                v = jnp.where(keep, v, jnp.zeros_like(v))
                acc = acc + jnp.dot(a.astype(BF), v, preferred_element_type=F32)
                return run + jnp.sum(l1mb, axis=1, keepdims=True), acc

            carry = step(0, (jnp.zeros((tq, 1), F32), jnp.zeros((tq, LANES), F32)), True)
            _, acc = lax.fori_loop(1, i + 1, functools.partial(step, masked=False), carry)
            o_pair = acc if o_pair is None else o_pair + acc
        o_ref[:, j * LANES:(j + 1) * LANES] = o_pair


def _sb_attn(q, kv, b, t, tq):
    return pl.pallas_call(
        functools.partial(_sb_attn_kernel, tq=tq),
        grid=(b, t // tq),
        in_specs=[pl.BlockSpec((tq, 256), lambda bi, i: (bi * (t // tq) + i, 0)),
                  pl.BlockSpec((t, 512), lambda bi, i: (bi, 0))],
        out_specs=pl.BlockSpec((tq, 256), lambda bi, i: (bi * (t // tq) + i, 0)),
        out_shape=jax.ShapeDtypeStruct((b * t, 256), F32),
        compiler_params=_cp(("parallel", "parallel")),
        name="sb_attn",
    )(q, kv)


def _gelu_tanh(x):
    return 0.5 * x * (1.0 + jnp.tanh(0.7978845608028654 * (x + 0.044715 * (x * x * x))))


def _compress_rows(get_rows, pe_ref, w1_ref, w2_ref):
    acc = None
    for j in range(0, CMP_BLOCK, 2):
        xa = get_rows(j) + pe_ref[j:j + 1, :]
        xb = get_rows(j + 1) + pe_ref[j + 1:j + 2, :]
        part = _bdot(jnp.concatenate([xa, xb], axis=1), w1_ref[j // 2])
        acc = part if acc is None else acc + part
    return _bdot(_gelu_tanh(acc), w2_ref[...])


def _compress_kernel(x_ref, pe_ref, w1_ref, w2_ref, o_ref):
    o_ref[...] = _compress_rows(lambda j: x_ref[:, j * LANES:(j + 1) * LANES], pe_ref, w1_ref, w2_ref)


def _compress(rows2, wl, tb):
    nb = rows2.shape[0]
    full = lambda a: pl.BlockSpec(a.shape, lambda i: (0,) * a.ndim)
    return pl.pallas_call(
        _compress_kernel,
        grid=(nb // tb,),
        in_specs=[pl.BlockSpec((tb, CMP_BLOCK * LANES), lambda i: (i, 0)), full(wl["pe"]), full(wl["w1"]), full(wl["w2"])],
        out_specs=pl.BlockSpec((tb, LANES), lambda i: (i, 0)),
        out_shape=jax.ShapeDtypeStruct((nb, LANES), F32),
        compiler_params=_cp(("parallel",)),
        name="nsa_compress",
    )(rows2, wl["pe"], wl["w1"], wl["w2"])


def _stack_heads(q384, tq):
    lo = _lane((tq, LANES)) < 64
    parts = []
    for j in range(3):
        slab = q384[:, j * LANES:(j + 1) * LANES].astype(F32)
        parts.append(jnp.where(lo, slab, jnp.zeros_like(slab)))
        parts.append(jnp.where(lo, pltpu.roll(slab, 64, axis=1), jnp.zeros_like(slab)))
    return jnp.concatenate(parts, axis=0)


def _unstack_heads(acc, tq):
    lo = _lane((tq, LANES)) < 64
    outs = []
    for j in range(3):
        a = acc[(2 * j) * tq:(2 * j + 1) * tq]
        b = acc[(2 * j + 1) * tq:(2 * j + 2) * tq]
        outs.append(jnp.where(lo, pltpu.roll(a, 64, axis=1), b))
    return jnp.concatenate(outs, axis=1)


def _cmp_select(q384, cb, qpos, tq, nb_true, ns_true):
    nbp = cb.shape[0]
    nsp = _round_up(nbp // RATIO, LANES)
    qs = _stack_heads(q384, tq)
    s = _bdot_nt(qs, cb) * HD ** -0.5
    blk = lax.broadcasted_iota(I32, (tq, nbp), 1)
    valid = jnp.logical_and((blk + 1) * CMP_BLOCK - 1 <= qpos, blk < nb_true)
    valid6 = _tile_mask(valid, NSA_H)
    sm = jnp.where(valid6, s, NEG)
    m = jnp.max(sm, axis=1, keepdims=True)
    e = jnp.where(valid6, jnp.exp(sm - m), 0.0)
    p = e / jnp.maximum(jnp.sum(e, axis=1, keepdims=True), 1.0)
    o = _unstack_heads(_bdot(p, cb), tq)
    ps = p[0:tq]
    for h in range(1, NSA_H):
        ps = ps + p[h * tq:(h + 1) * tq]
    pair = (lax.broadcasted_iota(I32, (nbp, nsp), 0) // RATIO == lax.broadcasted_iota(I32, (nbp, nsp), 1)).astype(BF)
    imp = _dot3(ps, pair)
    sblk = lax.broadcasted_iota(I32, (tq, nsp), 1)
    cur = qpos // SLC_BLOCK
    forced = jnp.logical_or(sblk == 0, sblk == cur)
    score = jnp.where(forced, FORCE_SCORE, jnp.where(sblk > cur, -1.0, imp))
    score = jnp.where(sblk < ns_true, score, -2.0)
    lanef = sblk.astype(F32)
    sel = jnp.zeros((tq, nsp), F32)
    for _ in range(min(N_SELECT, ns_true)):
        mx = jnp.max(score, axis=1, keepdims=True)
        first = jnp.min(jnp.where(score == mx, lanef, 1e9), axis=1, keepdims=True)
        hit = lanef == first
        sel = jnp.where(hit, 1.0, sel)
        score = jnp.where(hit, -3.0, score)
    return o, sel


def _cmp_prompt_kernel(q_ref, cb_ref, o_ref, sel_ref, *, tq, nb, ns):
    i = pl.program_id(1)
    qpos = i * tq + lax.broadcasted_iota(I32, (tq, 1), 0)
    o, sel = _cmp_select(q_ref[...], cb_ref[...], qpos, tq, nb, ns)
    o_ref[...] = o
    sel_ref[...] = sel


def _cmp_prompt(nq_raw, cblk, b, t, tq):
    nb = t // CMP_BLOCK
    ns = t // SLC_BLOCK
    nbp = _round_up(nb, LANES)
    nsp = _round_up(nbp // RATIO, LANES)
    cb = jnp.pad(cblk, ((0, 0), (0, nbp - nb), (0, 0))).reshape(b * nbp, LANES)
    return pl.pallas_call(
        functools.partial(_cmp_prompt_kernel, tq=tq, nb=nb, ns=ns),
        grid=(b, t // tq),
        in_specs=[pl.BlockSpec((tq, 384), lambda bi, i: (bi * (t // tq) + i, 0)),
                  pl.BlockSpec((nbp, LANES), lambda bi, i: (bi, 0))],
        out_specs=[pl.BlockSpec((tq, 384), lambda bi, i: (bi * (t // tq) + i, 0)),
                   pl.BlockSpec((tq, nsp), lambda bi, i: (bi * (t // tq) + i, 0))],
        out_shape=[jax.ShapeDtypeStruct((b * t, 384), F32), jax.ShapeDtypeStruct((b * t, nsp), F32)],
        compiler_params=_cp(("parallel", "parallel")),
        name="nsa_cmp_select",
    )(nq_raw, cb)


def _gate_merge(g, o_cmp, o_slc, o_win, tq):
    row = lax.broadcasted_iota(I32, (LANES, 384), 0)
    col = lax.broadcasted_iota(I32, (LANES, 384), 1)
    res = None
    for br, o in enumerate((o_cmp, o_slc, o_win)):
        pick = (row == 3 * (col // HD) + br).astype(BF)
        gm = _dot3(g, pick)
        res = gm * o if res is None else res + gm * o
    return res


def _nsa_attn_kernel(q_ref, slc_ref, win_ref, sel_ref, g_ref, ocmp_ref, o_ref, *, tq, ns):
    i = pl.program_id(1)
    scale = HD ** -0.5
    qs = _stack_heads(q_ref[...], tq)
    sel = sel_ref[...].astype(BF)
    rq = lax.broadcasted_iota(I32, (tq, tq), 0)
    ck = lax.broadcasted_iota(I32, (tq, tq), 1)
    erow = lax.broadcasted_iota(I32, (ns, tq), 0)
    ecol = lax.broadcasted_iota(I32, (ns, tq), 1)
    init = (jnp.full((NSA_H * tq, 1), NEG, F32), jnp.zeros((NSA_H * tq, 1), F32), jnp.zeros((NSA_H * tq, LANES), F32))

    def slc_step(kb, carry):
        ks = pl.multiple_of(kb * tq, tq)
        kv = slc_ref[pl.ds(ks, tq), :]
        s = _bdot_nt(qs, kv) * scale
        expand = ((kb * tq + ecol) // SLC_BLOCK == erow).astype(BF)
        chosen = jnp.dot(sel, expand, preferred_element_type=F32) > 0.5
        ok = jnp.logical_and(chosen, kb * tq + ck <= i * tq + rq)
        s = jnp.where(_tile_mask(ok, NSA_H), s, NEG)
        return _softmax_step(s, carry, kv)

    _, l, acc = lax.fori_loop(0, i + 1, slc_step, init)
    o_slc = _unstack_heads(acc / l, tq)

    def win_step(kb, carry):
        ks = pl.multiple_of(kb * tq, tq)
        kv = win_ref[pl.ds(ks, tq), :]
        s = _bdot_nt(qs, kv) * scale
        d = (i * tq + rq) - (kb * tq + ck)
        ok = jnp.logical_and(d >= 0, d < WINDOW)
        s = jnp.where(_tile_mask(ok, NSA_H), s, NEG)
        return _softmax_step(s, carry, kv)

    first = jnp.maximum(i - WINDOW // tq, 0)
    _, l, acc = lax.fori_loop(first, i + 1, win_step, init)
    o_win = _unstack_heads(acc / l, tq)
    o_ref[...] = _gate_merge(g_ref[...], ocmp_ref[...], o_slc, o_win, tq)


def _nsa_attn(nq_rot, slc_b, win_b, sel, gates, o_cmp, b, t, tq):
    ns = sel.shape[1]
    tile = lambda w: pl.BlockSpec((tq, w), lambda bi, i: (bi * (t // tq) + i, 0))
    seq = lambda w: pl.BlockSpec((t, w), lambda bi, i: (bi, 0))
    return pl.pallas_call(
        functools.partial(_nsa_attn_kernel, tq=tq, ns=ns),
        grid=(b, t // tq),
        in_specs=[tile(384), seq(LANES), seq(LANES), tile(ns), tile(LANES), tile(384)],
        out_specs=tile(384),
        out_shape=jax.ShapeDtypeStruct((b * t, 384), F32),
        compiler_params=_cp(("parallel", "parallel")),
        name="nsa_attn",
    )(nq_rot, slc_b, win_b, sel, gates, o_cmp)


def _rms(x, g):
    return x * lax.rsqrt(jnp.mean(x * x, -1, keepdims=True) + RMS_EPS) * g


def _layer_norm(x, g, b):
    mu = jnp.mean(x, -1, keepdims=True)
    xc = x - mu
    var = jnp.mean(xc * xc, -1, keepdims=True)
    return xc * lax.rsqrt(var + LN_EPS) * g + b


def _mixout_kernel(x_ref, om_ref, on_ref, os_ref, gn_ref, w_ref, lg_ref, lb_ref, o_ref):
    gn = gn_ref[...]
    y = (_bdot(_rms(om_ref[...], gn[:, 0:384]), w_ref[0:384, :])
         + _bdot(_rms(on_ref[...], gn[:, 384:768]), w_ref[384:768, :])
         + _bdot(_rms(os_ref[...], gn[:, 768:1024]), w_ref[768:1024, :]))
    o_ref[...] = _layer_norm(ALPHA * x_ref[...] + y, lg_ref[...], lb_ref[...])


def _mixout(x, o_mla, o_nsa, o_sb, wl, tm):
    n, d = x.shape
    tile = lambda w: pl.BlockSpec((tm, w), lambda i: (i, 0))
    full = lambda a: pl.BlockSpec(a.shape, lambda i: (0,) * a.ndim)
    return pl.pallas_call(
        _mixout_kernel,
        grid=(n // tm,),
        in_specs=[tile(d), tile(384), tile(384), tile(256), full(wl["grp_norm"]), full(wl["w_out"]),
                  full(wl["ln1_g"]), full(wl["ln1_b"])],
        out_specs=tile(d),
        out_shape=jax.ShapeDtypeStruct((n, d), F32),
        compiler_params=_cp(("parallel",)),
        name="mix_out_ln1",
    )(x, o_mla, o_nsa, o_sb, wl["grp_norm"], wl["w_out"], wl["ln1_g"], wl["ln1_b"])


def _router_kernel(x_ref, wh_ref, wl_ref, b_ref, idx_ref, rnk_ref, gw_ref, cnt_ref, *, tm):
    @pl.when(pl.program_id(0) == 0)
    def _():
        cnt_ref[...] = jnp.zeros_like(cnt_ref)

    x = x_ref[...]
    xh = x.astype(BF)
    xl = (x - xh.astype(F32)).astype(BF)
    logits = (jnp.dot(xh, wh_ref[...], preferred_element_type=F32) + jnp.dot(xh, wl_ref[...], preferred_element_type=F32)
              + jnp.dot(xl, wh_ref[...], preferred_element_type=F32))
    s = jax.nn.sigmoid(logits)
    score = s + b_ref[...]
    lanef = _lane((tm, N_EXP)).astype(F32)
    lane128 = _lane((tm, LANES))
    chosen = jnp.zeros((tm, N_EXP), F32)
    idx_out = jnp.zeros((tm, LANES), F32)
    val_out = jnp.zeros((tm, LANES), F32)
    hits = []
    for k in range(TOP_K):
        mx = jnp.max(score, axis=1, keepdims=True)
        first = jnp.min(jnp.where(score == mx, lanef, 1e9), axis=1, keepdims=True)
        hit = lanef == first
        hits.append(hit)
        chosen = jnp.where(hit, 1.0, chosen)
        score = jnp.where(hit, NEG, score)
        idx_out = jnp.where(lane128 == k, first, idx_out)
        val_out = jnp.where(lane128 == k, jnp.sum(jnp.where(hit, s, 0.0), axis=1, keepdims=True), val_out)
    tot = jnp.sum(val_out, axis=1, keepdims=True)
    gw_ref[...] = val_out / tot * ROUTED_SCALE
    idx_ref[...] = idx_out.astype(I32)
    before = (lax.broadcasted_iota(I32, (tm, tm), 1) < lax.broadcasted_iota(I32, (tm, tm), 0)).astype(BF)
    rank = jnp.dot(before, chosen.astype(BF), preferred_element_type=F32) + cnt_ref[...]
    rnk_out = jnp.zeros((tm, LANES), F32)
    for k in range(TOP_K):
        rnk_out = jnp.where(lane128 == k, jnp.sum(jnp.where(hits[k], rank, 0.0), axis=1, keepdims=True), rnk_out)
    rnk_ref[...] = rnk_out.astype(I32)
    cnt_ref[...] += jnp.sum(chosen, axis=0, keepdims=True)


def _router(x, wl, tm):
    n, d = x.shape
    full = lambda a: pl.BlockSpec(a.shape, lambda i: (0,) * a.ndim)
    tile = lambda w: pl.BlockSpec((tm, w), lambda i: (i, 0))
    return pl.pallas_call(
        functools.partial(_router_kernel, tm=tm),
        grid=(n // tm,),
        in_specs=[tile(d), full(wl["wr_hi"]), full(wl["wr_lo"]), full(wl["router_bias"])],
        out_specs=[tile(LANES), tile(LANES), tile(LANES), pl.BlockSpec((1, N_EXP), lambda i: (0, 0))],
        out_shape=[jax.ShapeDtypeStruct((n, LANES), I32), jax.ShapeDtypeStruct((n, LANES), I32),
                   jax.ShapeDtypeStruct((n, LANES), F32), jax.ShapeDtypeStruct((1, N_EXP), F32)],
        compiler_params=_cp(("arbitrary",)),
        name="moe_router",
    )(x, wl["wr_hi"], wl["wr_lo"], wl["router_bias"])


def _dispatch_kernel(pstart_ref, idx_ref, rnk_ref, x_hbm, dest_ref, xbuf_hbm, sem, *, tm, nsteps):
    i = pl.program_id(0)

    def wait_batch():
        pltpu.make_async_copy(xbuf_hbm.at[pl.ds(0, tm * TOP_K)], xbuf_hbm.at[pl.ds(0, tm * TOP_K)], sem).wait()

    def body(t, _):
        for k in range(TOP_K):
            j = t * TOP_K + k
            d = pstart_ref[idx_ref[j]] + rnk_ref[j]
            dest_ref[j] = d
            pltpu.make_async_copy(x_hbm.at[pl.ds(i * tm + t, 1)], xbuf_hbm.at[pl.ds(d, 1)], sem).start()
        return 0

    lax.fori_loop(0, tm, body, 0)

    @pl.when(i > 0)
    def _():
        wait_batch()

    @pl.when(i == nsteps - 1)
    def _():
        wait_batch()


def _dispatch(x, idx8, rnk8, pstart, n_rows, tm):
    n, d = x.shape
    nsteps = n // tm
    blk = pl.BlockSpec((tm * TOP_K,), lambda i, ps: (i,), memory_space=pltpu.SMEM)
    return pl.pallas_call(
        functools.partial(_dispatch_kernel, tm=tm, nsteps=nsteps),
        grid_spec=pltpu.PrefetchScalarGridSpec(
            num_scalar_prefetch=1,
            grid=(nsteps,),
            in_specs=[blk, blk, pl.BlockSpec(memory_space=pl.ANY)],
            out_specs=[blk, pl.BlockSpec(memory_space=pl.ANY)],
            scratch_shapes=[pltpu.SemaphoreType.DMA(())],
        ),
        out_shape=[jax.ShapeDtypeStruct((n * TOP_K,), I32), jax.ShapeDtypeStruct((n_rows, d), F32)],
        compiler_params=_cp(("arbitrary",)),
        name="moe_dispatch",
    )(pstart, idx8, rnk8, x)


def _experts_kernel(be_ref, nv_ref, x_ref, wg_ref, wu_ref, wd_ref, o_ref):
    b = pl.program_id(0)
    nv = nv_ref[b]

    @pl.when(nv > 0)
    def _():
        rows = lax.broadcasted_iota(I32, x_ref.shape, 0)
        x = jnp.where(rows < nv, x_ref[...], 0.0).astype(BF)
        hg = jnp.dot(x, wg_ref[...], preferred_element_type=F32)
        hu = jnp.dot(x, wu_ref[...], preferred_element_type=F32)
        o_ref[...] = _bdot(hg * jax.nn.sigmoid(hg) * hu, wd_ref[...])

    @pl.when(nv == 0)
    def _():
        o_ref[...] = jnp.zeros_like(o_ref)


def _experts(xbuf, blk_expert, blk_valid, wl):
    n_rows, d = xbuf.shape
    n_blk = n_rows // MOE_ROWS
    return pl.pallas_call(
        _experts_kernel,
        grid_spec=pltpu.PrefetchScalarGridSpec(
            num_scalar_prefetch=2,
            grid=(n_blk,),
            in_specs=[pl.BlockSpec((MOE_ROWS, d), lambda b, be, nv: (b, 0)),
                      pl.BlockSpec((None, d, D_EXP), lambda b, be, nv: (be[b], 0, 0)),
                      pl.BlockSpec((None, d, D_EXP), lambda b, be, nv: (be[b], 0, 0)),
                      pl.BlockSpec((None, D_EXP, d), lambda b, be, nv: (be[b], 0, 0))],
            out_specs=pl.BlockSpec((MOE_ROWS, d), lambda b, be, nv: (b, 0)),
        ),
        out_shape=jax.ShapeDtypeStruct((n_rows, d), F32),
        compiler_params=_cp(("arbitrary",)),
        name="moe_experts",
    )(blk_expert, blk_valid, xbuf, wl["moe_wg"], wl["moe_wu"], wl["moe_wd"])


def _combine_kernel(dcur_ref, dnxt_ref, x_ref, gw_ref, sg_ref, su_ref, sd_ref, lg_ref, lb_ref, ybuf_hbm,
                    o_ref, buf, sem, *, tm, nsteps):
    i = pl.program_id(0)
    slot = i % 2

    def issue(d_ref, sl):
        def body(t, _):
            for k in range(TOP_K):
                pltpu.make_async_copy(ybuf_hbm.at[pl.ds(d_ref[t * TOP_K + k], 1)],
                                      buf.at[sl, k, pl.ds(t, 1)], sem.at[sl]).start()
            return 0
        lax.fori_loop(0, tm, body, 0)

    @pl.when(i == 0)
    def _():
        issue(dcur_ref, 0)

    @pl.when(i + 1 < nsteps)
    def _():
        issue(dnxt_ref, 1 - slot)

    x = x_ref[...]
    hg = _bdot(x, sg_ref[...])
    shared = _bdot(hg * jax.nn.sigmoid(hg) * _bdot(x, su_ref[...]), sd_ref[...])
    pltpu.make_async_copy(buf.at[slot], buf.at[slot], sem.at[slot]).wait()
    gw = gw_ref[...]
    y = gw[:, 0:1] * buf[slot, 0]
    for k in range(1, TOP_K):
        y = y + gw[:, k:k + 1] * buf[slot, k]
    o_ref[...] = _layer_norm(ALPHA * x + (y + shared), lg_ref[...], lb_ref[...])


def _combine(x, dest, gw, ybuf, wl, tm):
    n, d = x.shape
    nsteps = n // tm
    tile = lambda w: pl.BlockSpec((tm, w), lambda i: (i, 0))
    full = lambda a: pl.BlockSpec(a.shape, lambda i: (0,) * a.ndim)
    return pl.pallas_call(
        functools.partial(_combine_kernel, tm=tm, nsteps=nsteps),
        grid=(nsteps,),
        in_specs=[pl.BlockSpec((tm * TOP_K,), lambda i: (i,), memory_space=pltpu.SMEM),
                  pl.BlockSpec((tm * TOP_K,), lambda i: (jnp.minimum(i + 1, nsteps - 1),), memory_space=pltpu.SMEM),
                  tile(d), tile(LANES), full(wl["sh_g"]), full(wl["sh_u"]), full(wl["sh_d"]),
                  full(wl["ln2_g"]), full(wl["ln2_b"]), pl.BlockSpec(memory_space=pl.ANY)],
        out_specs=tile(d),
        out_shape=jax.ShapeDtypeStruct((n, d), F32),
        scratch_shapes=[pltpu.VMEM((2, TOP_K, tm, d), F32), pltpu.SemaphoreType.DMA((2,))],
        compiler_params=_cp(("arbitrary",)),
        name="moe_combine_ln2",
    )(dest, dest, x, gw, wl["sh_g"], wl["sh_u"], wl["sh_d"], wl["ln2_g"], wl["ln2_b"], ybuf)


def _moe(x, wl):
    n, d = x.shape
    tm_r = min(512, n)
    idx, rnk, gw, cnt = _router(x, wl, tm_r)
    counts = cnt[0].astype(I32)
    padded = (counts + MOE_ROWS - 1) // MOE_ROWS * MOE_ROWS
    pend = jnp.cumsum(padded)
    pstart = (pend - padded).astype(I32)
    n_blk = (n * TOP_K + N_EXP * (MOE_ROWS - 1) + MOE_ROWS - 1) // MOE_ROWS
    blk0 = jnp.arange(n_blk, dtype=I32) * MOE_ROWS
    blk_expert = jnp.minimum(jnp.searchsorted(pend, blk0, side="right"), N_EXP - 1).astype(I32)
    blk_valid = jnp.clip(pstart[blk_expert] + counts[blk_expert] - blk0, 0, MOE_ROWS).astype(I32)
    idx8 = idx[:, :TOP_K].reshape(-1)
    rnk8 = rnk[:, :TOP_K].reshape(-1)
    dest, xbuf = _dispatch(x, idx8, rnk8, pstart, n_blk * MOE_ROWS, min(256, n))
    ybuf = _experts(xbuf, blk_expert, blk_valid, wl)
    return _combine(x, dest, gw, ybuf, wl, min(128, n))


def _page_fetch(cache_hbm, layer, pt_ref, slab, sem, slot, seq, chunk, pps, n_pages, start):
    for p in range(pps):
        page = pt_ref[seq * n_pages + chunk * pps + p]
        cp = pltpu.make_async_copy(cache_hbm.at[layer, page], slab.at[slot, pl.ds(p * PAGE, PAGE)], sem.at[slot])
        if start:
            cp.start()
        else:
            cp.wait()


def _paged_pipeline(cache_hbm, layer, pt_ref, slab, sem, pps, n_pages, nch, nseq, reverse):
    b = pl.program_id(0)
    c = pl.program_id(1)
    step = b * nch + c
    slot = step % 2
    order = (lambda cc: nch - 1 - cc) if reverse else (lambda cc: cc)

    @pl.when(step == 0)
    def _():
        _page_fetch(cache_hbm, layer, pt_ref, slab, sem, 0, 0, order(0), pps, n_pages, True)

    @pl.when(step + 1 < nseq * nch)
    def _():
        nxt = step + 1
        _page_fetch(cache_hbm, layer, pt_ref, slab, sem, 1 - slot, nxt // nch, order(nxt % nch), pps, n_pages, True)

    _page_fetch(cache_hbm, layer, pt_ref, slab, sem, slot, b, order(c), pps, n_pages, False)
    return slot


def _mla_sample_kernel(pt_ref, q_ref, new_ref, wukt_ref, wuv_ref, cache_hbm, o_ref,
                       slab, sem, qa_s, qr_s, m_s, l_s, acc_s, kr_s, *, layer, pps, n_pages, nch, nseq, s_new):
    c = pl.program_id(1)
    slot = _paged_pipeline(cache_hbm, layer, pt_ref, slab, sem, pps, n_pages, nch, nseq, False)
    scale = (MLA_NOPE + MLA_ROPE) ** -0.5
    rows = MLA_H * s_new

    @pl.when(c == 0)
    def _():
        q = q_ref[0]
        pick = (lax.broadcasted_iota(I32, (LANES, LANES), 0) == lax.broadcasted_iota(I32, (LANES, LANES), 1) + 64).astype(BF)
        qa, qr = [], []
        for h in range(MLA_H):
            qh = q[:, h * LANES:(h + 1) * LANES]
            qa.append(jnp.dot(qh, wukt_ref[h], preferred_element_type=F32))
            qr.append(jnp.dot(qh, pick, preferred_element_type=F32))
        qa_s[...] = jnp.concatenate(qa, axis=0)
        qr_s[...] = jnp.concatenate(qr, axis=0)
        m_s[...] = jnp.full(m_s.shape, NEG, F32)
        l_s[...] = jnp.zeros_like(l_s)
        acc_s[...] = jnp.zeros_like(acc_s)
        kr_s[...] = jnp.zeros_like(kr_s)

    qa = qa_s[...]
    qr = qr_s[...]

    def attend(ckv, kr128, mask):
        s = (_bdot_nt(qa, ckv) + _bdot_nt(qr, kr128)) * scale
        if mask is not None:
            s = jnp.where(mask, s, NEG)
        m, l, acc = _softmax_step(s, (m_s[...], l_s[...], acc_s[...]), ckv.astype(BF))
        m_s[...] = m
        l_s[...] = l
        acc_s[...] = acc

    nk = pps * PAGE
    kr_s[0:nk, 0:MLA_ROPE] = slab[slot, :, MLA_KVL:MLA_ROW]
    attend(slab[slot, :, 0:MLA_KVL], kr_s[0:nk, :], None)

    @pl.when(c == nch - 1)
    def _():
        kr_s[0:PAGE, 0:MLA_ROPE] = new_ref[0, :, MLA_KVL:MLA_ROW]
        qi = lax.broadcasted_iota(I32, (rows, PAGE), 0) % s_new
        attend(new_ref[0, :, 0:MLA_KVL], kr_s[0:PAGE, :], lax.broadcasted_iota(I32, (rows, PAGE), 1) <= qi)
        o_lat = acc_s[...] / l_s[...]
        o = None
        for h in range(MLA_H):
            part = _bdot(o_lat[h * s_new:(h + 1) * s_new], wuv_ref[h])
            o = part if o is None else o + part
        o_ref[0] = o


def _mla_sample(q, new_rows, cache, pt_flat, wl, layer, nseq, s_new, n_pages, pps):
    nch = n_pages // pps
    rows = MLA_H * s_new
    full = lambda a: pl.BlockSpec(a.shape, lambda b, c, pt: (0,) * a.ndim)
    return pl.pallas_call(
        functools.partial(_mla_sample_kernel, layer=layer, pps=pps, n_pages=n_pages, nch=nch, nseq=nseq, s_new=s_new),
        grid_spec=pltpu.PrefetchScalarGridSpec(
            num_scalar_prefetch=1,
            grid=(nseq, nch),
            in_specs=[pl.BlockSpec((1, s_new, 768), lambda b, c, pt: (b, 0, 0)),
                      pl.BlockSpec((1, PAGE, MLA_ROW), lambda b, c, pt: (b, 0, 0)),
                      full(wl["wukt"]), full(wl["wuv_h"]), pl.BlockSpec(memory_space=pl.ANY)],
            out_specs=pl.BlockSpec((1, s_new, 384), lambda b, c, pt: (b, 0, 0)),
            scratch_shapes=[pltpu.VMEM((2, pps * PAGE, MLA_ROW), F32), pltpu.SemaphoreType.DMA((2,)),
                            pltpu.VMEM((rows, MLA_KVL), F32), pltpu.VMEM((rows, LANES), F32),
                            pltpu.VMEM((rows, 1), F32), pltpu.VMEM((rows, 1), F32), pltpu.VMEM((rows, MLA_KVL), F32),
                            pltpu.VMEM((pps * PAGE, LANES), F32)],
        ),
        out_shape=jax.ShapeDtypeStruct((nseq, s_new, 384), F32),
        compiler_params=_cp(("arbitrary", "arbitrary")),
        name="mla_sample",
    )(pt_flat, q, new_rows, wl["wukt"], wl["wuv_h"], cache)


def _rev_cumsum_excl(x):
    lane = _lane(x.shape)
    inc = x
    sh = 1
    while sh < LANES:
        inc = inc + jnp.where(lane + sh < LANES, pltpu.roll(inc, LANES - sh, axis=1), 0.0)
        sh *= 2
    return inc - x


def _sb_sample_kernel(pt_ref, q_ref, new_ref, cache_hbm, o_ref, slab, sem, run_s, acc_s,
                      *, layer, pps, n_pages, nch, nseq, s_new):
    c = pl.program_id(1)
    slot = _paged_pipeline(cache_hbm, layer, pt_ref, slab, sem, pps, n_pages, nch, nseq, True)
    scale = HD ** -0.5
    rows = SB_H * s_new
    q = q_ref[0].astype(F32)
    own = lax.broadcasted_iota(I32, (rows, 256), 0) // s_new == lax.broadcasted_iota(I32, (rows, 256), 1) // HD
    q4 = jnp.where(own, _tile_r(q, SB_H), 0.0)

    @pl.when(c == 0)
    def _():
        new = new_ref[0]
        z = _bdot_nt(q4, new[:, 0:256]) * scale
        l1mb, lsz = _sb_terms(z)
        qi = lax.broadcasted_iota(I32, (rows, PAGE), 0) % s_new
        strict = lax.broadcasted_iota(I32, (rows, PAGE), 1) < qi
        l1mb = jnp.where(strict, l1mb, 0.0)
        a = jnp.where(strict, jnp.exp(lsz + _rev_cumsum_excl(l1mb)), 0.0)
        acc_s[...] = _bdot(a, new[:, 256:512])
        run_s[...] = jnp.sum(l1mb, axis=1, keepdims=True)

    past = slab[slot]
    z = _bdot_nt(q4, past[:, 0:256]) * scale
    l1mb, lsz = _sb_terms(z)
    run = run_s[...]
    pieces = [None] * pps
    for p in range(pps - 1, -1, -1):
        blk = l1mb[:, p * PAGE:(p + 1) * PAGE]
        pieces[p] = _rev_cumsum_excl(blk) + run
        run = run + jnp.sum(blk, axis=1, keepdims=True)
    run_s[...] = run
    a = jnp.exp(lsz + jnp.concatenate(pieces, axis=1))
    acc_s[...] += _bdot(a, past[:, 256:512])

    @pl.when(c == nch - 1)
    def _():
        res = jnp.where(own, acc_s[...], 0.0)
        o = res[0:s_new]
        for h in range(1, SB_H):
            o = o + res[h * s_new:(h + 1) * s_new]
        o_ref[0] = o


def _sb_sample(q, new_rows, cache, pt_flat, layer, nseq, s_new, n_pages, pps):
    nch = n_pages // pps
    rows = SB_H * s_new
    return pl.pallas_call(
        functools.partial(_sb_sample_kernel, layer=layer, pps=pps, n_pages=n_pages, nch=nch, nseq=nseq, s_new=s_new),
        grid_spec=pltpu.PrefetchScalarGridSpec(
            num_scalar_prefetch=1,
            grid=(nseq, nch),
            in_specs=[pl.BlockSpec((1, s_new, 256), lambda b, c, pt: (b, 0, 0)),
                      pl.BlockSpec((1, PAGE, 512), lambda b, c, pt: (b, 0, 0)),
                      pl.BlockSpec(memory_space=pl.ANY)],
            out_specs=pl.BlockSpec((1, s_new, 256), lambda b, c, pt: (b, 0, 0)),
            scratch_shapes=[pltpu.VMEM((2, pps * PAGE, 512), F32), pltpu.SemaphoreType.DMA((2,)),
                            pltpu.VMEM((rows, 1), F32), pltpu.VMEM((rows, 256), F32)],
        ),
        out_shape=jax.ShapeDtypeStruct((nseq, s_new, 256), F32),
        compiler_params=_cp(("arbitrary", "arbitrary")),
        name="sb_sample",
    )(pt_flat, q, new_rows, cache)


def _cmp_sample_kernel(pt_ref, q_ref, cnew_ref, pe_ref, w1_ref, w2_ref, cache_hbm, o_ref, sel_ref, slab, sem,
                       *, layer, n_pages, nseq, s_new, past_len, nb_true, ns_true):
    slot = _paged_pipeline(cache_hbm, layer, pt_ref, slab, sem, n_pages, n_pages, 1, nseq, False)
    nb_past = past_len // CMP_BLOCK
    cb_past = _compress_rows(lambda j: slab[slot, pl.ds(j, nb_past, stride=CMP_BLOCK), :], pe_ref, w1_ref, w2_ref)
    cb = jnp.concatenate([cb_past, cnew_ref[0]], axis=0)
    qpos = past_len + lax.broadcasted_iota(I32, (s_new, 1), 0)
    o, sel = _cmp_select(q_ref[0], cb, qpos, s_new, nb_true, ns_true)
    o_ref[0] = o
    sel_ref[0] = sel


def _cmp_sample(nq_raw, cnew, cache, pt_flat, wl, layer, nseq, s_new, n_pages, nb_true, ns_true):
    past_len = n_pages * PAGE
    nbp = past_len // CMP_BLOCK + cnew.shape[1]
    assert nbp % LANES == 0
    nsp = _round_up(nbp // RATIO, LANES)
    full = lambda a: pl.BlockSpec(a.shape, lambda b, c, pt: (0,) * a.ndim)
    return pl.pallas_call(
        functools.partial(_cmp_sample_kernel, layer=layer, n_pages=n_pages, nseq=nseq, s_new=s_new,
                          past_len=past_len, nb_true=nb_true, ns_true=ns_true),
        grid_spec=pltpu.PrefetchScalarGridSpec(
            num_scalar_prefetch=1,
            grid=(nseq, 1),
            in_specs=[pl.BlockSpec((1, s_new, 384), lambda b, c, pt: (b, 0, 0)),
                      pl.BlockSpec((1, cnew.shape[1], LANES), lambda b, c, pt: (b, 0, 0)),
                      full(wl["pe"]), full(wl["w1"]), full(wl["w2"]), pl.BlockSpec(memory_space=pl.ANY)],
            out_specs=[pl.BlockSpec((1, s_new, 384), lambda b, c, pt: (b, 0, 0)),
                       pl.BlockSpec((1, s_new, nsp), lambda b, c, pt: (b, 0, 0))],
            scratch_shapes=[pltpu.VMEM((2, past_len, LANES), F32), pltpu.SemaphoreType.DMA((2,))],
        ),
        out_shape=[jax.ShapeDtypeStruct((nseq, s_new, 384), F32), jax.ShapeDtypeStruct((nseq, s_new, nsp), F32)],
        compiler_params=_cp(("arbitrary", "arbitrary")),
        name="nsa_cmp_sample",
    )(pt_flat, nq_raw, cnew, wl["pe"], wl["w1"], wl["w2"], cache)


def _nsa_sample_kernel(pt_ref, q_ref, sel_ref, slcn_ref, winn_ref, winbuf_ref, g_ref, ocmp_ref, cache_hbm, o_ref,
                       slab, sem, *, layer, n_pages, nseq, s_new, past_len):
    slot = _paged_pipeline(cache_hbm, layer, pt_ref, slab, sem, n_pages, n_pages, 1, nseq, False)
    scale = HD ** -0.5
    rows = NSA_H * s_new
    qs = _stack_heads(q_ref[0], s_new)
    sel = sel_ref[0]
    nsp = sel.shape[1]
    nsel_past = past_len // SLC_BLOCK
    qi = lax.broadcasted_iota(I32, (s_new, PAGE), 0)
    ki = lax.broadcasted_iota(I32, (s_new, PAGE), 1)
    init = (jnp.full((rows, 1), NEG, F32), jnp.zeros((rows, 1), F32), jnp.zeros((rows, LANES), F32))
    kv = slab[slot].astype(BF)
    expand = (lax.broadcasted_iota(I32, (nsp, past_len), 1) // SLC_BLOCK == lax.broadcasted_iota(I32, (nsp, past_len), 0)).astype(BF)
    chosen = jnp.dot(sel.astype(BF), expand, preferred_element_type=F32) > 0.5
    s = jnp.where(_tile_mask(chosen, NSA_H), _bdot_nt(qs, kv) * scale, NEG)
    carry = _softmax_step(s, init, kv)
    kvn = slcn_ref[0].astype(BF)
    new_sel = jnp.sum(jnp.where(_lane(sel.shape) == nsel_past, sel, 0.0), axis=1, keepdims=True) > 0.5
    ok = jnp.logical_and(new_sel, ki <= qi)
    s = jnp.where(_tile_mask(ok, NSA_H), _bdot_nt(qs, kvn) * scale, NEG)
    _, l, acc = _softmax_step(s, carry, kvn)
    o_slc = _unstack_heads(acc / l, s_new)
    wb = winbuf_ref[0].astype(BF)
    wr = wb.shape[0]
    jpos = lax.broadcasted_iota(I32, (s_new, wr), 1)
    qrow = lax.broadcasted_iota(I32, (s_new, wr), 0)
    ok = (jpos - wr) > (qrow - WINDOW)
    s = jnp.where(_tile_mask(ok, NSA_H), _bdot_nt(qs, wb) * scale, NEG)
    carry = _softmax_step(s, init, wb)
    wn = winn_ref[0].astype(BF)
    s = jnp.where(_tile_mask(ki <= qi, NSA_H), _bdot_nt(qs, wn) * scale, NEG)
    _, l, acc = _softmax_step(s, carry, wn)
    o_win = _unstack_heads(acc / l, s_new)
    o_ref[0] = _gate_merge(g_ref[0], ocmp_ref[0], o_slc, o_win, s_new)


def _nsa_sample(nq_rot, sel, slc_new, win_new, win_buf, gates, o_cmp, cache, pt_flat, layer, nseq, s_new, n_pages):
    past_len = n_pages * PAGE
    per_seq = lambda a: pl.BlockSpec((1,) + a.shape[1:], lambda b, c, pt: (b,) + (0,) * (a.ndim - 1))
    return pl.pallas_call(
        functools.partial(_nsa_sample_kernel, layer=layer, n_pages=n_pages, nseq=nseq, s_new=s_new, past_len=past_len),
        grid_spec=pltpu.PrefetchScalarGridSpec(
            num_scalar_prefetch=1,
            grid=(nseq, 1),
            in_specs=[per_seq(nq_rot), per_seq(sel), per_seq(slc_new), per_seq(win_new), per_seq(win_buf),
                      per_seq(gates), per_seq(o_cmp), pl.BlockSpec(memory_space=pl.ANY)],
            out_specs=pl.BlockSpec((1, s_new, 384), lambda b, c, pt: (b, 0, 0)),
            scratch_shapes=[pltpu.VMEM((2, past_len, LANES), F32), pltpu.SemaphoreType.DMA((2,))],
        ),
        out_shape=jax.ShapeDtypeStruct((nseq, s_new, 384), F32),
        compiler_params=_cp(("arbitrary", "arbitrary")),
        name="nsa_sample",
    )(pt_flat, nq_rot, sel, slc_new, win_new, win_buf, gates, o_cmp, cache)


def _rot_half(w, half):
    return jnp.concatenate([-w[..., half:], w[..., :half]], axis=-1)


def _rope_tables(pos):
    pos = pos.astype(F32)[:, None]
    t = pos.shape[0]

    def cs(half):
        inv = 1.0 / (ROPE_THETA ** (jnp.arange(half, dtype=F32) / half))
        ang = pos * inv[None, :]
        return jnp.cos(ang), jnp.sin(ang)

    c16, s16 = cs(MLA_ROPE // 2)
    c32, s32 = cs(HD // 2)
    one = lambda w: jnp.ones((t, w), F32)
    zero = lambda w: jnp.zeros((t, w), F32)
    cq = jnp.concatenate([one(64), c16, c16, zero(32)], 1)
    sq = jnp.concatenate([zero(64), s16, s16, zero(32)], 1)
    ckr = jnp.concatenate([c16, c16, zero(32), c16, c16, zero(32)], 1)
    skr = jnp.concatenate([s16, s16, zero(32), s16, s16, zero(32)], 1)
    cn = jnp.concatenate([c32, c32, c32, c32], 1)
    sn = jnp.concatenate([s32, s32, s32, s32], 1)
    c2 = jnp.concatenate([c32, c32, one(64)], 1)
    s2 = jnp.concatenate([s32, s32, zero(64)], 1)
    return jnp.concatenate([cq, sq, ckr, skr, cn, sn, c2, s2], 1)


def _prep_layer(l, w_in, mla_q_norm, mla_w_uq, mla_kv_norm, mla_w_uk, mla_w_uv, nsa_cmp_pe, nsa_cmp_w1, nsa_cmp_w2,
                grp_norm, w_out, ln1_g, ln1_b, w_router, router_bias, moe_w_gate, moe_w_up, moe_w_down,
                sh_w_gate, sh_w_up, sh_w_down, ln2_g, ln2_b):
    w = w_in[l]
    d = w.shape[0]
    z = lambda n: jnp.zeros((d, n), F32)
    kr = w[:, 640:672]
    krr = _rot_half(kr, MLA_ROPE // 2)
    nq = w[:, 672:1056]
    nqp = _rot_half(nq.reshape(d, NSA_H, HD), HD // 2).reshape(d, NSA_H * HD)
    nkv = w[:, 1056:1440]
    slcp = jnp.concatenate([_rot_half(w[:, 1184:1248], HD // 2), z(64)], 1)
    winp = jnp.concatenate([_rot_half(w[:, 1312:1376], HD // 2), z(64)], 1)
    gates = jnp.concatenate([w[:, 1440:1458], z(LANES - 18)], 1)
    w_ext = jnp.concatenate([w[:, 0:384], w[:, 384:640], kr, z(32), kr, z(32), krr, z(32), krr, z(32),
                             nq, nqp, nkv, slcp, winp, gates, w[:, 1458:2226]], 1).astype(BF)
    assert w_ext.shape[1] == W_EXT
    uq = mla_w_uq[l]
    r = uq.shape[0]
    zq = lambda n: jnp.zeros((r, MLA_H, n), F32)
    slab1 = jnp.concatenate([uq, zq(32)], -1).reshape(r, MLA_H * LANES)
    slab2 = jnp.concatenate([zq(64), _rot_half(uq[..., MLA_NOPE:], MLA_ROPE // 2), zq(32)], -1).reshape(r, MLA_H * LANES)
    uk = mla_w_uk[l]
    uv = mla_w_uv[l]
    wuk = jnp.concatenate([uk, jnp.zeros_like(uk)], -1).reshape(MLA_KVL, MLA_H * LANES)
    ukt = jnp.transpose(uk, (1, 2, 0))
    wukt = jnp.concatenate([ukt, jnp.zeros_like(ukt)], 1)
    eye = jnp.eye(MLA_H, dtype=F32)
    wuv_h = jnp.einsum("rhv,hg->hrgv", uv, eye).reshape(MLA_H, MLA_KVL, MLA_H * HD)
    w1 = nsa_cmp_w1[l][:, 0]
    w1 = w1.reshape(2, CMP_BLOCK, HD, HD)
    zz = jnp.zeros((CMP_BLOCK, HD, HD), F32)
    w1bd = jnp.concatenate([jnp.concatenate([w1[0], zz], -1), jnp.concatenate([zz, w1[1]], -1)], 1)
    w1bd = w1bd.reshape(CMP_BLOCK // 2, 2 * LANES, LANES)
    w2 = nsa_cmp_w2[l][:, 0]
    z2 = jnp.zeros((HD, HD), F32)
    w2bd = jnp.concatenate([jnp.concatenate([w2[0], z2], -1), jnp.concatenate([z2, w2[1]], -1)], 0)
    wr = w_router[l]
    wr_hi = wr.astype(BF)
    return {
        "w_ext": w_ext, "q_norm": mla_q_norm[l][None], "wuq": jnp.concatenate([slab1, slab2], 1).astype(BF),
        "kv_norm": mla_kv_norm[l][None], "wuk": wuk.astype(BF), "wuv": uv.reshape(MLA_KVL, MLA_H * HD).astype(BF),
        "wukt": wukt.astype(BF), "wuv_h": wuv_h.astype(BF),
        "pe": nsa_cmp_pe[l].reshape(CMP_BLOCK, LANES), "w1": w1bd.astype(BF), "w2": w2bd.astype(BF),
        "grp_norm": grp_norm[l][None], "w_out": w_out[l].astype(BF), "ln1_g": ln1_g[l][None], "ln1_b": ln1_b[l][None],
        "wr_hi": wr_hi, "wr_lo": (wr - wr_hi.astype(F32)).astype(BF), "router_bias": router_bias[l][None],
        "moe_wg": moe_w_gate[l].astype(BF), "moe_wu": moe_w_up[l].astype(BF), "moe_wd": moe_w_down[l].astype(BF),
        "sh_g": sh_w_gate[l].astype(BF), "sh_u": sh_w_up[l].astype(BF), "sh_d": sh_w_down[l].astype(BF),
        "ln2_g": ln2_g[l][None], "ln2_b": ln2_b[l][None],
    }


def _pick(n, pref):
    t = min(pref, n)
    while n % t:
        t //= 2
    return t


def _prompt_layer(hp, wl, tab, b, t):
    n = b * t
    tm = _pick(t, 512)
    (rows, q, k, v, nqr, nqt, cmp_r, slc_r, win_r, slc_b, win_b, gates, sbq, sb_r, sb_b) = _proj(hp, wl, tab, tm, t // tm)
    tq = _pick(t, 256)
    o_mla = _mla_attn(q, k, v, b, t, tq)
    o_sb = _sb_attn(sbq, sb_b, b, t, tq)
    nblk = n // CMP_BLOCK
    cblk = _compress(cmp_r.reshape(nblk, CMP_BLOCK * LANES), wl, _pick(nblk, 256))
    tqn = _pick(t, 128)
    o_cmp, sel = _cmp_prompt(nqr, cblk.reshape(b, t // CMP_BLOCK, LANES), b, t, tqn)
    o_nsa = _nsa_attn(nqt, slc_b, win_b, sel, gates, o_cmp, b, t, tqn)
    h1 = _mixout(hp, o_mla, o_nsa, o_sb, wl, _pick(n, 512))
    h2 = _moe(h1, wl)
    return h2, rows, cmp_r, slc_r, win_r, sb_r


def _sample_layer(hs, wl, tab, l, caches, pt_flat, win_buf, db, s, n_pages):
    cache_mla, cache_cmp, cache_slc, cache_sb = caches
    n = db * s
    past_len = n_pages * PAGE
    tm = _pick(n, 512)
    (rows, q, k, v, nqr, nqt, cmp_r, slc_r, win_r, slc_b, win_b, gates, sbq, sb_r, sb_b) = _proj(hs, wl, tab, tm, 1)
    seq = lambda a: a.reshape(db, s, a.shape[-1])
    keys = lambda a: jnp.pad(seq(a), ((0, 0), (0, PAGE - s), (0, 0)))
    o_mla = _mla_sample(seq(q), keys(rows), cache_mla, pt_flat, wl, l, db, s, n_pages, _pick(n_pages, 32))
    o_sb = _sb_sample(seq(sbq), keys(sb_r), cache_sb, pt_flat, l, db, s, n_pages, _pick(n_pages, 16))
    sp = -(-s // SLC_BLOCK) * SLC_BLOCK
    cmp_pad = jnp.pad(seq(cmp_r), ((0, 0), (0, sp - s), (0, 0)))
    nb_new = sp // CMP_BLOCK
    cnew = _compress(cmp_pad.reshape(db * nb_new, CMP_BLOCK * LANES), wl, _pick(db * nb_new, 256)).reshape(db, nb_new, LANES)
    nb_true = past_len // CMP_BLOCK + nb_new
    ns_true = nb_true // RATIO
    cnew = jnp.pad(cnew, ((0, 0), (0, (-nb_true) % LANES), (0, 0)))
    o_cmp, sel = _cmp_sample(seq(nqr), cnew, cache_cmp, pt_flat, wl, l, db, s, n_pages, nb_true, ns_true)
    o_nsa = _nsa_sample(seq(nqt), sel, keys(slc_r), keys(win_r), win_buf, seq(gates), o_cmp,
                        cache_slc, pt_flat, l, db, s, n_pages)
    h1 = _mixout(hs, o_mla.reshape(n, 384), o_nsa.reshape(n, 384), o_sb.reshape(n, 256), wl, _pick(n, 512))
    h2 = _moe(h1, wl)
    return h2, rows, cmp_r, slc_r, win_r, sb_r


def kernel(x_prompt, x_sample, cache_mla, cache_nsa_cmp, cache_nsa_slc, state_nsa_win, cache_sb, page_table, w_in, mla_q_norm, mla_w_uq, mla_kv_norm, mla_w_uk, mla_w_uv, nsa_cmp_pe, nsa_cmp_w1, nsa_cmp_w2, grp_norm, w_out, ln1_g, ln1_b, w_router, router_bias, moe_w_gate, moe_w_up, moe_w_down, sh_w_gate, sh_w_up, sh_w_down, ln2_g, ln2_b):
    b, t, d = x_prompt.shape
    db, s, _ = x_sample.shape
    n_pages = page_table.shape[1]
    past_len = n_pages * PAGE
    depth = w_in.shape[0]
    weights = (w_in, mla_q_norm, mla_w_uq, mla_kv_norm, mla_w_uk, mla_w_uv, nsa_cmp_pe, nsa_cmp_w1, nsa_cmp_w2,
               grp_norm, w_out, ln1_g, ln1_b, w_router, router_bias, moe_w_gate, moe_w_up, moe_w_down,
               sh_w_gate, sh_w_up, sh_w_down, ln2_g, ln2_b)
    tab_p = _rope_tables(jnp.arange(t))
    tm_s = _pick(db * s, 512)
    tab_s = jnp.tile(_rope_tables(past_len + jnp.arange(s)), (tm_s // s, 1))
    pt_flat = page_table.reshape(-1).astype(I32)
    rows2d = lambda c: c.reshape(c.shape[:3] + (-1,))
    caches = (cache_mla, rows2d(cache_nsa_cmp), rows2d(cache_nsa_slc), rows2d(cache_sb))
    hp = x_prompt.reshape(b * t, d)
    hs = x_sample.reshape(db * s, d)
    outs = [[] for _ in range(10)]
    wr = state_nsa_win.shape[2]
    keep = min(WINDOW, wr + s)
    for l in range(depth):
        wl = _prep_layer(l, *weights)
        hp, rows, cmp_r, slc_r, win_r, sb_r = _prompt_layer(hp, wl, tab_p, b, t)
        outs[0].append(rows.reshape(b, t, MLA_ROW))
        outs[2].append(cmp_r.reshape(b, t, 2, 1, HD))
        outs[4].append(slc_r.reshape(b, t, 2, 1, HD))
        outs[6].append(win_r.reshape(b, t, 2, 1, HD)[:, t - min(WINDOW, t):])
        outs[8].append(sb_r.reshape(b, t, 2, SB_H, HD))
        win_buf = state_nsa_win[l].reshape(db, wr, LANES)
        hs, rows, cmp_r, slc_r, win_r, sb_r = _sample_layer(hs, wl, tab_s, l, caches, pt_flat, win_buf, db, s, n_pages)
        outs[1].append(rows.reshape(db, s, MLA_ROW))
        outs[3].append(cmp_r.reshape(db, s, 2, 1, HD))
        outs[5].append(slc_r.reshape(db, s, 2, 1, HD))
        new_win = jnp.concatenate([win_buf, win_r.reshape(db, s, LANES)], axis=1)[:, wr + s - keep:]
        outs[7].append(new_win.reshape(db, keep, 2, 1, HD))
        outs[9].append(sb_r.reshape(db, s, 2, SB_H, HD))
    st = [jnp.stack(o) for o in outs]
    return (hp.reshape(b, t, d), hs.reshape(db, s, d), st[0], st[1], st[2], st[3], st[4], st[5], st[6], st[7], st[8], st[9])
```

```python
import functools

import jax
import jax.numpy as jnp
from jax import lax
from jax.experimental import pallas as pl
from jax.experimental.pallas import tpu as pltpu

F32 = jnp.float32
BF = jnp.bfloat16
I32 = jnp.int32

PAGE = 128
HD = 64
MLA_H = 6
MLA_QL = 384
MLA_KVL = 256
MLA_NOPE = 64
MLA_ROPE = 32
MLA_ROW = MLA_KVL + MLA_ROPE
NSA_H = 6
CMP_BLOCK = 32
SLC_BLOCK = 64
RATIO = SLC_BLOCK // CMP_BLOCK
N_SELECT = 8
WINDOW = 512
FORCE_SCORE = 1.0e4
SB_H = 4
N_EXP = 64
TOP_K = 8
D_EXP = 256
ROUTED_SCALE = 2.5
ROPE_THETA = 10000.0
LN_EPS = 1e-5
RMS_EPS = 1e-6
DEPTH = 2
ALPHA = (2 * DEPTH) ** 0.25
IN_SPLITS = (MLA_QL, MLA_ROW, NSA_H * HD, 6 * HD, 3 * NSA_H, 3 * SB_H * HD)
W_EXT = 3200
NEG = -1e30

LANES = 128
VMEM_LIMIT = 56 * 1024 * 1024
MOE_ROWS = 256
MLA_TQ, MLA_TK = 256, 512
NSA_TK = 512


def _cp(sem):
    return pltpu.CompilerParams(dimension_semantics=sem, vmem_limit_bytes=VMEM_LIMIT)


def _bdot(a, b):
    return jnp.dot(a.astype(BF), b.astype(BF), preferred_element_type=F32)


def _bdot_nt(a, b):
    return lax.dot_general(a.astype(BF), b.astype(BF), (((1,), (1,)), ((), ())), preferred_element_type=F32)


def _split3(x):
    hi = x.astype(BF)
    r1 = x - hi.astype(F32)
    mid = r1.astype(BF)
    lo = (r1 - mid.astype(F32)).astype(BF)
    return hi, mid, lo


def _dot3(x, m01):
    hi, mid, lo = _split3(x)
    return (jnp.dot(hi, m01, preferred_element_type=F32) + jnp.dot(mid, m01, preferred_element_type=F32)
            + jnp.dot(lo, m01, preferred_element_type=F32))


def _lane(shape):
    return lax.broadcasted_iota(I32, shape, len(shape) - 1)


def _tile_l(x, n):
    return jnp.concatenate([x] * n, axis=1)


def _tile_r(x, n):
    return jnp.concatenate([x] * n, axis=0)


def _tile_mask(ok, n):
    return _tile_r(ok.astype(I32), n) > 0


def _round_up(n, m):
    return -(-n // m) * m


def _proj_kernel(x_ref, w_ref, qn_ref, wuq_ref, kvn_ref, wuk_ref, wuv_ref, tab_ref,
                 rows_ref, q_ref, k_ref, v_ref, nqr_ref, nqt_ref, cmp_ref, slc_ref, win_ref,
                 slcb_ref, winb_ref, g_ref, sbq_ref, sbr_ref, sbb_ref):
    h = _bdot(x_ref[...], w_ref[...])
    tab = tab_ref[...]
    cq, sq, ckr, skr, cn, sn, c2, s2 = [tab[:, i * LANES:(i + 1) * LANES] for i in range(8)]
    qd = h[:, 0:384]
    qd = qd * lax.rsqrt(jnp.mean(qd * qd, -1, keepdims=True) + RMS_EPS) * qn_ref[...]
    ql = _bdot(qd, wuq_ref[...])
    q_ref[...] = (ql[:, :768] * _tile_l(cq, 6) + ql[:, 768:] * _tile_l(sq, 6)).astype(BF)
    ckv = h[:, 384:640]
    ckv = ckv * lax.rsqrt(jnp.mean(ckv * ckv, -1, keepdims=True) + RMS_EPS) * kvn_ref[...]
    kr = h[:, 640:768] * ckr + h[:, 768:896] * skr
    rows_ref[:, 0:MLA_KVL] = ckv
    rows_ref[:, MLA_KVL:MLA_ROW] = kr[:, 0:MLA_ROPE]
    kr_hi = jnp.where(_lane(kr.shape) >= 64, kr, 0.0)
    k_ref[...] = (_bdot(ckv, wuk_ref[...]) + _tile_l(kr_hi, 6)).astype(BF)
    v_ref[...] = _bdot(ckv, wuv_ref[...]).astype(BF)
    nq = h[:, 896:1280]
    nqr_ref[...] = nq.astype(BF)
    nqt_ref[...] = (nq * _tile_l(cn, 3) + h[:, 1280:1664] * _tile_l(sn, 3)).astype(BF)
    cmp_ref[...] = h[:, 1664:1792]
    slc = h[:, 1792:1920] * c2 + h[:, 2048:2176] * s2
    win = h[:, 1920:2048] * c2 + h[:, 2176:2304] * s2
    slc_ref[...] = slc
    win_ref[...] = win
    slcb_ref[...] = slc.astype(BF)
    winb_ref[...] = win.astype(BF)
    g_ref[...] = jax.nn.sigmoid(h[:, 2304:2432])
    sbq_ref[...] = h[:, 2432:2688].astype(BF)
    sbr = h[:, 2688:3200]
    sbr_ref[...] = sbr
    sbb_ref[...] = sbr.astype(BF)


def _proj(x, wl, tab, tm, n_tab_blocks):
    n, d = x.shape
    outs = [
        ((n, MLA_ROW), F32), ((n, 768), BF), ((n, 768), BF), ((n, 384), BF),
        ((n, 384), BF), ((n, 384), BF), ((n, 128), F32), ((n, 128), F32), ((n, 128), F32),
        ((n, 128), BF), ((n, 128), BF), ((n, 128), F32), ((n, 256), BF), ((n, 512), F32), ((n, 512), BF),
    ]
    full = lambda a: pl.BlockSpec(a.shape, lambda i: (0,) * a.ndim)
    return pl.pallas_call(
        _proj_kernel,
        grid=(n // tm,),
        in_specs=[pl.BlockSpec((tm, d), lambda i: (i, 0)), full(wl["w_ext"]), full(wl["q_norm"]), full(wl["wuq"]),
                  full(wl["kv_norm"]), full(wl["wuk"]), full(wl["wuv"]),
                  pl.BlockSpec((tm, 1024), lambda i: (i % n_tab_blocks, 0))],
        out_specs=[pl.BlockSpec((tm, s[1]), lambda i: (i, 0)) for s, _ in outs],
        out_shape=[jax.ShapeDtypeStruct(s, dt) for s, dt in outs],
        compiler_params=_cp(("parallel",)),
        name="proj",
    )(x, wl["w_ext"], wl["q_norm"], wl["wuq"], wl["kv_norm"], wl["wuk"], wl["wuv"], tab)


def _softmax_step(s, carry, v, vt=False):
    m, l, acc = carry
    m_new = jnp.maximum(m, jnp.max(s, axis=1, keepdims=True))
    alpha = jnp.exp(m - m_new)
    p = jnp.exp(s - m_new)
    l = alpha * l + jnp.sum(p, axis=1, keepdims=True)
    pv = _bdot_nt(p, v) if vt else jnp.dot(p.astype(BF), v, preferred_element_type=F32)
    return m_new, l, alpha * acc + pv


def _mla_attn_kernel(q_ref, k_ref, v_ref, o_ref, *, tq, tk):
    i = pl.program_id(1)
    scale = (MLA_NOPE + MLA_ROPE) ** -0.5
    lo_q = _lane((tq, LANES)) < 64
    lo_k = _lane((tk, LANES)) < 64
    rq = lax.broadcasted_iota(I32, (tq, tk), 0)
    ck = lax.broadcasted_iota(I32, (tq, tk), 1)

    def step(kb, carry, masked):
        ks = pl.multiple_of(kb * tk, tk)
        if masked:
            ok = kb * tk + ck <= i * tq + rq
        out = []
        for j in range(MLA_H // 2):
            (m_a, l_a), (m_b, l_b), acc = carry[j]
            v = v_ref[pl.ds(ks, tk), j * LANES:(j + 1) * LANES]
            zero = jnp.zeros_like(v)
            halves = []
            for hh, (m, l) in enumerate(((m_a, l_a), (m_b, l_b))):
                h = 2 * j + hh
                s = _bdot_nt(q_ref[:, h * LANES:(h + 1) * LANES], k_ref[pl.ds(ks, tk), h * LANES:(h + 1) * LANES]) * scale
                if masked:
                    s = jnp.where(ok, s, NEG)
                m_new = jnp.maximum(m, jnp.max(s, axis=1, keepdims=True))
                alpha = jnp.exp(m - m_new)
                p = jnp.exp(s - m_new)
                l = alpha * l + jnp.sum(p, axis=1, keepdims=True)
                vm = jnp.where(lo_k, v, zero) if hh == 0 else jnp.where(lo_k, zero, v)
                halves.append((m_new, l, alpha, jnp.dot(p.astype(BF), vm, preferred_element_type=F32)))
            (ma, la, aa, pva), (mb, lb, ab, pvb) = halves
            acc = jnp.where(lo_q, aa, ab) * acc + (pva + pvb)
            out.append(((ma, la), (mb, lb), acc))
        return tuple(out)

    ml = (jnp.full((tq, 1), NEG, F32), jnp.zeros((tq, 1), F32))
    init = tuple((ml, ml, jnp.zeros((tq, LANES), F32)) for _ in range(MLA_H // 2))
    n_full = (i * tq) // tk
    n_all = ((i + 1) * tq + tk - 1) // tk
    carry = lax.fori_loop(0, n_full, functools.partial(step, masked=False), init)
    carry = lax.fori_loop(n_full, n_all, functools.partial(step, masked=True), carry)
    for j in range(MLA_H // 2):
        (_, l_a), (_, l_b), acc = carry[j]
        o_ref[:, j * LANES:(j + 1) * LANES] = acc / jnp.where(lo_q, l_a, l_b)


def _mla_attn(q, k, v, b, t, tq, tk):
    return pl.pallas_call(
        functools.partial(_mla_attn_kernel, tq=tq, tk=tk),
        grid=(b, t // tq),
        in_specs=[pl.BlockSpec((tq, 768), lambda bi, i: (bi * (t // tq) + i, 0)),
                  pl.BlockSpec((t, 768), lambda bi, i: (bi, 0)),
                  pl.BlockSpec((t, 384), lambda bi, i: (bi, 0))],
        out_specs=pl.BlockSpec((tq, 384), lambda bi, i: (bi * (t // tq) + i, 0)),
        out_shape=jax.ShapeDtypeStruct((b * t, 384), F32),
        compiler_params=_cp(("parallel", "parallel")),
        name="mla_attn",
    )(q, k, v)


def _sb_terms(z):
    lg = jnp.log1p(jnp.exp(-jnp.abs(z)))
    return -(jnp.maximum(z, 0.0) + lg), -(jnp.maximum(-z, 0.0) + lg)


def _sb_attn_kernel(q_ref, kv_ref, o_ref, *, tq):
    i = pl.program_id(1)
    lo = _lane((tq, LANES)) < 64
    r = lax.broadcasted_iota(I32, (tq, tq), 0)
    c = lax.broadcasted_iota(I32, (tq, tq), 1)
    strict = c < r
    upper = (r > c).astype(BF)
    qs = []
    for j in range(SB_H // 2):
        qp = q_ref[:, j * LANES:(j + 1) * LANES] * HD ** -0.5
        qs.append((jnp.where(lo, qp, jnp.zeros_like(qp)), jnp.where(lo, jnp.zeros_like(qp), qp)))

    def step(n, carry, masked):
        kb = i - n
        ks = pl.multiple_of(kb * tq, tq)
        out = []
        for j in range(SB_H // 2):
            run_a, run_b, acc = carry[j]
            k = kv_ref[pl.ds(ks, tq), j * LANES:(j + 1) * LANES]
            v = kv_ref[pl.ds(ks, tq), 256 + j * LANES:256 + (j + 1) * LANES]
            zero = jnp.zeros_like(v)
            runs = []
            for hh, run in enumerate((run_a, run_b)):
                z = _bdot_nt(qs[j][hh], k)
                l1mb, lsz = _sb_terms(z)
                if masked:
                    l1mb = jnp.where(strict, l1mb, 0.0)
                a = jnp.exp(lsz + (_dot3(l1mb, upper) + run))
                if masked:
                    a = jnp.where(strict, a, 0.0)
                vm = jnp.where(lo, v, zero) if hh == 0 else jnp.where(lo, zero, v)
                acc = acc + jnp.dot(a.astype(BF), vm, preferred_element_type=F32)
                runs.append(run + jnp.sum(l1mb, axis=1, keepdims=True))
            out.append((runs[0], runs[1], acc))
        return tuple(out)

    zr = jnp.zeros((tq, 1), F32)
    carry = step(0, tuple((zr, zr, jnp.zeros((tq, LANES), F32)) for _ in range(SB_H // 2)), True)
    carry = lax.fori_loop(1, i + 1, functools.partial(step, masked=False), carry)
    for j in range(SB_H // 2):
        o_ref[:, j * LANES:(j + 1) * LANES] = carry[j][2]


def _sb_attn(q, kv, b, t, tq):
    return pl.pallas_call(
        functools.partial(_sb_attn_kernel, tq=tq),
        grid=(b, t // tq),
        in_specs=[pl.BlockSpec((tq, 256), lambda bi, i: (bi * (t // tq) + i, 0)),
                  pl.BlockSpec((t, 512), lambda bi, i: (bi, 0))],
        out_specs=pl.BlockSpec((tq, 256), lambda bi, i: (bi * (t // tq) + i, 0)),
        out_shape=jax.ShapeDtypeStruct((b * t, 256), F32),
        compiler_params=_cp(("parallel", "parallel")),
        name="sb_attn",
    )(q, kv)


def _gelu_tanh(x):
    return 0.5 * x * (1.0 + jnp.tanh(0.7978845608028654 * (x + 0.044715 * (x * x * x))))


def _compress_rows(get_rows, pe_ref, w1_ref, w2_ref):
    acc = None
    for j in range(0, CMP_BLOCK, 2):
        xa = get_rows(j) + pe_ref[j:j + 1, :]
        xb = get_rows(j + 1) + pe_ref[j + 1:j + 2, :]
        part = _bdot(jnp.concatenate([xa, xb], axis=1), w1_ref[j // 2])
        acc = part if acc is None else acc + part
    return _bdot(_gelu_tanh(acc), w2_ref[...])


def _compress_kernel(x_ref, pe_ref, w1_ref, w2_ref, o_ref):
    o_ref[...] = _compress_rows(lambda j: x_ref[:, j * LANES:(j + 1) * LANES], pe_ref, w1_ref, w2_ref)


def _compress(rows2, wl, tb):
    nb = rows2.shape[0]
    full = lambda a: pl.BlockSpec(a.shape, lambda i: (0,) * a.ndim)
    return pl.pallas_call(
        _compress_kernel,
        grid=(nb // tb,),
        in_specs=[pl.BlockSpec((tb, CMP_BLOCK * LANES), lambda i: (i, 0)), full(wl["pe"]), full(wl["w1"]), full(wl["w2"])],
        out_specs=pl.BlockSpec((tb, LANES), lambda i: (i, 0)),
        out_shape=jax.ShapeDtypeStruct((nb, LANES), F32),
        compiler_params=_cp(("parallel",)),
        name="nsa_compress",
    )(rows2, wl["pe"], wl["w1"], wl["w2"])


def _stack_heads(q384, tq):
    lo = _lane((tq, LANES)) < 64
    parts = []
    for j in range(3):
        slab = q384[:, j * LANES:(j + 1) * LANES].astype(F32)
        parts.append(jnp.where(lo, slab, jnp.zeros_like(slab)))
        parts.append(jnp.where(lo, pltpu.roll(slab, 64, axis=1), jnp.zeros_like(slab)))
    return jnp.concatenate(parts, axis=0)


def _unstack_heads(acc, tq):
    lo = _lane((tq, LANES)) < 64
    outs = []
    for j in range(3):
        a = acc[(2 * j) * tq:(2 * j + 1) * tq]
        b = acc[(2 * j + 1) * tq:(2 * j + 2) * tq]
        outs.append(jnp.where(lo, pltpu.roll(a, 64, axis=1), b))
    return jnp.concatenate(outs, axis=1)


def _cmp_select(q384, cb, qpos, tq, nb_true, ns_true):
    nbp = cb.shape[0]
    nsp = _round_up(nbp // RATIO, LANES)
    qs = _stack_heads(q384, tq)
    s = _bdot_nt(qs, cb) * HD ** -0.5
    blk = lax.broadcasted_iota(I32, (tq, nbp), 1)
    valid = jnp.logical_and((blk + 1) * CMP_BLOCK - 1 <= qpos, blk < nb_true)
    valid6 = _tile_mask(valid, NSA_H)
    sm = jnp.where(valid6, s, NEG)
    m = jnp.max(sm, axis=1, keepdims=True)
    e = jnp.where(valid6, jnp.exp(sm - m), 0.0)
    p = e / jnp.maximum(jnp.sum(e, axis=1, keepdims=True), 1.0)
    o = _unstack_heads(_bdot(p, cb), tq)
    ps = p[0:tq]
    for h in range(1, NSA_H):
        ps = ps + p[h * tq:(h + 1) * tq]
    pair = (lax.broadcasted_iota(I32, (nbp, nsp), 0) // RATIO == lax.broadcasted_iota(I32, (nbp, nsp), 1)).astype(BF)
    imp = _dot3(ps, pair)
    sblk = lax.broadcasted_iota(I32, (tq, nsp), 1)
    cur = qpos // SLC_BLOCK
    forced = jnp.logical_or(sblk == 0, sblk == cur)
    score = jnp.where(forced, FORCE_SCORE, jnp.where(sblk > cur, -1.0, imp))
    score = jnp.where(sblk < ns_true, score, -2.0)
    lanef = sblk.astype(F32)
    sel = jnp.zeros((tq, nsp), F32)
    for _ in range(min(N_SELECT, ns_true)):
        mx = jnp.max(score, axis=1, keepdims=True)
        first = jnp.min(jnp.where(score == mx, lanef, 1e9), axis=1, keepdims=True)
        hit = lanef == first
        sel = jnp.where(hit, 1.0, sel)
        score = jnp.where(hit, -3.0, score)
    return o, sel


def _cmp_prompt_kernel(q_ref, cb_ref, o_ref, sel_ref, *, tq, nb, ns):
    i = pl.program_id(1)
    qpos = i * tq + lax.broadcasted_iota(I32, (tq, 1), 0)
    o, sel = _cmp_select(q_ref[...], cb_ref[...], qpos, tq, nb, ns)
    o_ref[...] = o
    sel_ref[...] = sel


def _cmp_prompt(nq_raw, cblk, b, t, tq):
    nb = t // CMP_BLOCK
    ns = t // SLC_BLOCK
    nbp = _round_up(nb, LANES)
    nsp = _round_up(nbp // RATIO, LANES)
    cb = jnp.pad(cblk, ((0, 0), (0, nbp - nb), (0, 0))).reshape(b * nbp, LANES)
    return pl.pallas_call(
        functools.partial(_cmp_prompt_kernel, tq=tq, nb=nb, ns=ns),
        grid=(b, t // tq),
        in_specs=[pl.BlockSpec((tq, 384), lambda bi, i: (bi * (t // tq) + i, 0)),
                  pl.BlockSpec((nbp, LANES), lambda bi, i: (bi, 0))],
        out_specs=[pl.BlockSpec((tq, 384), lambda bi, i: (bi * (t // tq) + i, 0)),
                   pl.BlockSpec((tq, nsp), lambda bi, i: (bi * (t // tq) + i, 0))],
        out_shape=[jax.ShapeDtypeStruct((b * t, 384), F32), jax.ShapeDtypeStruct((b * t, nsp), F32)],
        compiler_params=_cp(("parallel", "parallel")),
        name="nsa_cmp_select",
    )(nq_raw, cb)


def _gate_merge(g, o_cmp, o_slc, o_win, tq):
    row = lax.broadcasted_iota(I32, (LANES, 384), 0)
    col = lax.broadcasted_iota(I32, (LANES, 384), 1)
    res = None
    for br, o in enumerate((o_cmp, o_slc, o_win)):
        pick = (row == 3 * (col // HD) + br).astype(BF)
        gm = _dot3(g, pick)
        res = gm * o if res is None else res + gm * o
    return res


def _nsa_attn_kernel(q_ref, slc_ref, win_ref, sel_ref, g_ref, ocmp_ref, o_ref, *, tq, tk, ns):
    i = pl.program_id(1)
    qs = (_stack_heads(q_ref[...], tq) * HD ** -0.5).astype(BF)
    sel = sel_ref[...].astype(BF)
    rq = lax.broadcasted_iota(I32, (tq, tk), 0)
    ck = lax.broadcasted_iota(I32, (tq, tk), 1)
    erow = lax.broadcasted_iota(I32, (ns, tk), 0)
    ecol = lax.broadcasted_iota(I32, (ns, tk), 1)
    init = (jnp.full((NSA_H * tq, 1), NEG, F32), jnp.zeros((NSA_H * tq, 1), F32), jnp.zeros((NSA_H * tq, LANES), F32))

    def attend(kv, ok, carry):
        bias = jnp.where(ok, 0.0, NEG)
        s = _bdot_nt(qs, kv).reshape(NSA_H, tq, tk) + bias[None]
        return _softmax_step(s.reshape(NSA_H * tq, tk), carry, kv)

    def slc_part(kb, carry):
        ks = pl.multiple_of(kb * tk, tk)
        expand = ((kb * tk + ecol) // SLC_BLOCK == erow).astype(BF)
        chosen = jnp.dot(sel, expand, preferred_element_type=F32) > 0.5
        ok = jnp.logical_and(chosen, kb * tk + ck <= i * tq + rq)
        return attend(slc_ref[pl.ds(ks, tk), :], ok, carry)

    def win_part(kb, carry):
        ks = pl.multiple_of(kb * tk, tk)
        d = (i * tq + rq) - (kb * tk + ck)
        return attend(win_ref[pl.ds(ks, tk), :], jnp.logical_and(d >= 0, d < WINDOW), carry)

    n_all = ((i + 1) * tq + tk - 1) // tk
    w_first = jnp.maximum(i * tq - (WINDOW - 1), 0) // tk
    c_slc = lax.fori_loop(0, w_first, slc_part, init)
    c_slc, c_win = lax.fori_loop(w_first, n_all, lambda kb, c: (slc_part(kb, c[0]), win_part(kb, c[1])), (c_slc, init))
    o_slc = _unstack_heads(c_slc[2] / c_slc[1], tq)
    o_win = _unstack_heads(c_win[2] / c_win[1], tq)
    o_ref[...] = _gate_merge(g_ref[...], ocmp_ref[...], o_slc, o_win, tq)


def _nsa_attn(nq_rot, slc_b, win_b, sel, gates, o_cmp, b, t, tq, tk):
    ns = sel.shape[1]
    tile = lambda w: pl.BlockSpec((tq, w), lambda bi, i: (bi * (t // tq) + i, 0))
    seq = lambda w: pl.BlockSpec((t, w), lambda bi, i: (bi, 0))
    return pl.pallas_call(
        functools.partial(_nsa_attn_kernel, tq=tq, tk=tk, ns=ns),
        grid=(b, t // tq),
        in_specs=[tile(384), seq(LANES), seq(LANES), tile(ns), tile(LANES), tile(384)],
        out_specs=tile(384),
        out_shape=jax.ShapeDtypeStruct((b * t, 384), F32),
        compiler_params=_cp(("parallel", "parallel")),
        name="nsa_attn",
    )(nq_rot, slc_b, win_b, sel, gates, o_cmp)


def _rms(x, g):
    return x * lax.rsqrt(jnp.mean(x * x, -1, keepdims=True) + RMS_EPS) * g


def _layer_norm(x, g, b):
    mu = jnp.mean(x, -1, keepdims=True)
    xc = x - mu
    var = jnp.mean(xc * xc, -1, keepdims=True)
    return xc * lax.rsqrt(var + LN_EPS) * g + b


def _mixout_kernel(x_ref, om_ref, on_ref, os_ref, gn_ref, w_ref, lg_ref, lb_ref, o_ref):
    gn = gn_ref[...]
    y = (_bdot(_rms(om_ref[...], gn[:, 0:384]), w_ref[0:384, :])
         + _bdot(_rms(on_ref[...], gn[:, 384:768]), w_ref[384:768, :])
         + _bdot(_rms(os_ref[...], gn[:, 768:1024]), w_ref[768:1024, :]))
    o_ref[...] = _layer_norm(ALPHA * x_ref[...] + y, lg_ref[...], lb_ref[...])


def _mixout(x, o_mla, o_nsa, o_sb, wl, tm):
    n, d = x.shape
    tile = lambda w: pl.BlockSpec((tm, w), lambda i: (i, 0))
    full = lambda a: pl.BlockSpec(a.shape, lambda i: (0,) * a.ndim)
    return pl.pallas_call(
        _mixout_kernel,
        grid=(n // tm,),
        in_specs=[tile(d), tile(384), tile(384), tile(256), full(wl["grp_norm"]), full(wl["w_out"]),
                  full(wl["ln1_g"]), full(wl["ln1_b"])],
        out_specs=tile(d),
        out_shape=jax.ShapeDtypeStruct((n, d), F32),
        compiler_params=_cp(("parallel",)),
        name="mix_out_ln1",
    )(x, o_mla, o_nsa, o_sb, wl["grp_norm"], wl["w_out"], wl["ln1_g"], wl["ln1_b"])


def _router_kernel(x_ref, w_ref, b_ref, idx_ref, rnk_ref, gw_ref, cnt_ref, *, tm):
    @pl.when(pl.program_id(0) == 0)
    def _():
        cnt_ref[...] = jnp.zeros_like(cnt_ref)

    x = x_ref[...]
    xh = x.astype(BF)
    xl = (x - xh.astype(F32)).astype(BF)
    w = w_ref[...]
    logits = jnp.dot(xh, w, preferred_element_type=F32) + jnp.dot(xl, w, preferred_element_type=F32)
    s = jax.nn.sigmoid(logits)
    score = s + b_ref[...]
    lanef = _lane((tm, N_EXP)).astype(F32)
    lane128 = _lane((tm, LANES))
    chosen = jnp.zeros((tm, N_EXP), F32)
    idx_out = jnp.zeros((tm, LANES), F32)
    val_out = jnp.zeros((tm, LANES), F32)
    hits = []
    for k in range(TOP_K):
        mx = jnp.max(score, axis=1, keepdims=True)
        first = jnp.min(jnp.where(score == mx, lanef, 1e9), axis=1, keepdims=True)
        hit = lanef == first
        hits.append(hit)
        chosen = jnp.where(hit, 1.0, chosen)
        score = jnp.where(hit, NEG, score)
        idx_out = jnp.where(lane128 == k, first, idx_out)
        val_out = jnp.where(lane128 == k, jnp.sum(jnp.where(hit, s, 0.0), axis=1, keepdims=True), val_out)
    tot = jnp.sum(val_out, axis=1, keepdims=True)
    gw_ref[...] = val_out / tot * ROUTED_SCALE
    idx_ref[...] = idx_out.astype(I32)
    before = (lax.broadcasted_iota(I32, (tm, tm), 1) < lax.broadcasted_iota(I32, (tm, tm), 0)).astype(BF)
    rank = jnp.dot(before, chosen.astype(BF), preferred_element_type=F32) + cnt_ref[...]
    rnk_out = jnp.zeros((tm, LANES), F32)
    for k in range(TOP_K):
        rnk_out = jnp.where(lane128 == k, jnp.sum(jnp.where(hits[k], rank, 0.0), axis=1, keepdims=True), rnk_out)
    rnk_ref[...] = rnk_out.astype(I32)
    cnt_ref[...] += jnp.sum(chosen, axis=0, keepdims=True)


def _router(x, wl, tm):
    n, d = x.shape
    full = lambda a: pl.BlockSpec(a.shape, lambda i: (0,) * a.ndim)
    tile = lambda w: pl.BlockSpec((tm, w), lambda i: (i, 0))
    return pl.pallas_call(
        functools.partial(_router_kernel, tm=tm),
        grid=(n // tm,),
        in_specs=[tile(d), full(wl["w_router"]), full(wl["router_bias"])],
        out_specs=[tile(LANES), tile(LANES), tile(LANES), pl.BlockSpec((1, N_EXP), lambda i: (0, 0))],
        out_shape=[jax.ShapeDtypeStruct((n, LANES), I32), jax.ShapeDtypeStruct((n, LANES), I32),
                   jax.ShapeDtypeStruct((n, LANES), F32), jax.ShapeDtypeStruct((1, N_EXP), F32)],
        compiler_params=_cp(("arbitrary",)),
        name="moe_router",
    )(x, wl["w_router"], wl["router_bias"])


def _dispatch_kernel(pstart_ref, idx_ref, rnk_ref, x_ref, dest_ref, xbuf_hbm, sem, *, tm):
    def body(t, _):
        for k in range(TOP_K):
            j = t * TOP_K + k
            d = pstart_ref[idx_ref[j]] + rnk_ref[j]
            dest_ref[j] = d
            pltpu.make_async_copy(x_ref.at[pl.ds(t, 1)], xbuf_hbm.at[pl.ds(d, 1)], sem).start()
        return 0

    lax.fori_loop(0, tm, body, 0)
    pltpu.make_async_copy(xbuf_hbm.at[pl.ds(0, tm * TOP_K)], xbuf_hbm.at[pl.ds(0, tm * TOP_K)], sem).wait()


def _dispatch(x, idx8, rnk8, pstart, n_rows, tm):
    n, d = x.shape
    nsteps = n // tm
    blk = pl.BlockSpec((tm * TOP_K,), lambda i, ps: (i,), memory_space=pltpu.SMEM)
    return pl.pallas_call(
        functools.partial(_dispatch_kernel, tm=tm),
        grid_spec=pltpu.PrefetchScalarGridSpec(
            num_scalar_prefetch=1,
            grid=(nsteps,),
            in_specs=[blk, blk, pl.BlockSpec((tm, d), lambda i, ps: (i, 0))],
            out_specs=[blk, pl.BlockSpec(memory_space=pl.ANY)],
            scratch_shapes=[pltpu.SemaphoreType.DMA(())],
        ),
        out_shape=[jax.ShapeDtypeStruct((n * TOP_K,), I32), jax.ShapeDtypeStruct((n_rows, d), F32)],
        compiler_params=_cp(("arbitrary",)),
        name="moe_dispatch",
    )(pstart, idx8, rnk8, x)


def _experts_kernel(be_ref, nv_ref, x_ref, wg_ref, wu_ref, wd_ref, o_ref):
    b = pl.program_id(0)
    nv = nv_ref[b]

    @pl.when(nv > 0)
    def _():
        rows = lax.broadcasted_iota(I32, x_ref.shape, 0)
        x = jnp.where(rows < nv, x_ref[...], 0.0).astype(BF)
        hg = jnp.dot(x, wg_ref[...], preferred_element_type=F32)
        hu = jnp.dot(x, wu_ref[...], preferred_element_type=F32)
        o_ref[...] = _bdot(hg * jax.nn.sigmoid(hg) * hu, wd_ref[...])

    @pl.when(nv == 0)
    def _():
        o_ref[...] = jnp.zeros_like(o_ref)


def _experts(xbuf, blk_expert, blk_valid, wl):
    n_rows, d = xbuf.shape
    n_blk = n_rows // MOE_ROWS
    return pl.pallas_call(
        _experts_kernel,
        grid_spec=pltpu.PrefetchScalarGridSpec(
            num_scalar_prefetch=2,
            grid=(n_blk,),
            in_specs=[pl.BlockSpec((MOE_ROWS, d), lambda b, be, nv: (b, 0)),
                      pl.BlockSpec((None, d, D_EXP), lambda b, be, nv: (be[b], 0, 0)),
                      pl.BlockSpec((None, d, D_EXP), lambda b, be, nv: (be[b], 0, 0)),
                      pl.BlockSpec((None, D_EXP, d), lambda b, be, nv: (be[b], 0, 0))],
            out_specs=pl.BlockSpec((MOE_ROWS, d), lambda b, be, nv: (b, 0)),
        ),
        out_shape=jax.ShapeDtypeStruct((n_rows, d), F32),
        compiler_params=_cp(("arbitrary",)),
        name="moe_experts",
    )(blk_expert, blk_valid, xbuf, wl["moe_wg"], wl["moe_wu"], wl["moe_wd"])


def _combine_kernel(dcur_ref, dnxt_ref, x_ref, gw_ref, sg_ref, su_ref, sd_ref, lg_ref, lb_ref, ybuf_hbm,
                    o_ref, buf, sem, *, tm, nsteps):
    i = pl.program_id(0)
    slot = i % 2

    def issue(d_ref, sl):
        def body(t, _):
            for k in range(TOP_K):
                pltpu.make_async_copy(ybuf_hbm.at[pl.ds(d_ref[t * TOP_K + k], 1)],
                                      buf.at[sl, k, pl.ds(t, 1)], sem.at[sl]).start()
            return 0
        lax.fori_loop(0, tm, body, 0)

    @pl.when(i == 0)
    def _():
        issue(dcur_ref, 0)

    @pl.when(i + 1 < nsteps)
    def _():
        issue(dnxt_ref, 1 - slot)

    x = x_ref[...]
    hg = _bdot(x, sg_ref[...])
    shared = _bdot(hg * jax.nn.sigmoid(hg) * _bdot(x, su_ref[...]), sd_ref[...])
    pltpu.make_async_copy(buf.at[slot], buf.at[slot], sem.at[slot]).wait()
    gw = gw_ref[...]
    y = gw[:, 0:1] * buf[slot, 0]
    for k in range(1, TOP_K):
        y = y + gw[:, k:k + 1] * buf[slot, k]
    o_ref[...] = _layer_norm(ALPHA * x + (y + shared), lg_ref[...], lb_ref[...])


def _combine(x, dest, gw, ybuf, wl, tm):
    n, d = x.shape
    nsteps = n // tm
    tile = lambda w: pl.BlockSpec((tm, w), lambda i: (i, 0))
    full = lambda a: pl.BlockSpec(a.shape, lambda i: (0,) * a.ndim)
    return pl.pallas_call(
        functools.partial(_combine_kernel, tm=tm, nsteps=nsteps),
        grid=(nsteps,),
        in_specs=[pl.BlockSpec((tm * TOP_K,), lambda i: (i,), memory_space=pltpu.SMEM),
                  pl.BlockSpec((tm * TOP_K,), lambda i: (jnp.minimum(i + 1, nsteps - 1),), memory_space=pltpu.SMEM),
                  tile(d), tile(LANES), full(wl["sh_g"]), full(wl["sh_u"]), full(wl["sh_d"]),
                  full(wl["ln2_g"]), full(wl["ln2_b"]), pl.BlockSpec(memory_space=pl.ANY)],
        out_specs=tile(d),
        out_shape=jax.ShapeDtypeStruct((n, d), F32),
        scratch_shapes=[pltpu.VMEM((2, TOP_K, tm, d), F32), pltpu.SemaphoreType.DMA((2,))],
        compiler_params=_cp(("arbitrary",)),
        name="moe_combine_ln2",
    )(dest, dest, x, gw, wl["sh_g"], wl["sh_u"], wl["sh_d"], wl["ln2_g"], wl["ln2_b"], ybuf)


def _moe(x, wl):
    n, d = x.shape
    tm_r = min(512, n)
    idx, rnk, gw, cnt = _router(x, wl, tm_r)
    counts = cnt[0].astype(I32)
    padded = (counts + MOE_ROWS - 1) // MOE_ROWS * MOE_ROWS
    pend = jnp.cumsum(padded)
    pstart = (pend - padded).astype(I32)
    n_blk = (n * TOP_K + N_EXP * (MOE_ROWS - 1) + MOE_ROWS - 1) // MOE_ROWS
    blk0 = jnp.arange(n_blk, dtype=I32) * MOE_ROWS
    blk_expert = jnp.minimum(jnp.sum((blk0[:, None] >= pend[None, :]).astype(I32), axis=1), N_EXP - 1)
    blk_valid = jnp.clip(pstart[blk_expert] + counts[blk_expert] - blk0, 0, MOE_ROWS).astype(I32)
    idx8 = idx[:, :TOP_K].reshape(-1)
    rnk8 = rnk[:, :TOP_K].reshape(-1)
    dest, xbuf = _dispatch(x, idx8, rnk8, pstart, n_blk * MOE_ROWS, min(512, n))
    ybuf = _experts(xbuf, blk_expert, blk_valid, wl)
    return _combine(x, dest, gw, ybuf, wl, min(128, n))


def _page_fetch(cache_hbm, layer, pt_ref, slab, sem, slot, seq, chunk, pps, n_pages, start, lanes):
    for p in range(pps):
        page = pt_ref[seq * n_pages + chunk * pps + p]
        dst = slab.at[slot, :, pl.ds(p * PAGE, PAGE)] if lanes else slab.at[slot, pl.ds(p * PAGE, PAGE)]
        cp = pltpu.make_async_copy(cache_hbm.at[layer, page], dst, sem.at[slot])
        if start:
            cp.start()
        else:
            cp.wait()


def _paged_pipeline(cache_hbm, layer, pt_ref, slab, sem, pps, n_pages, nch, nseq, reverse, lanes=True):
    b = pl.program_id(0)
    c = pl.program_id(1)
    step = b * nch + c
    slot = step % 2
    order = (lambda cc: nch - 1 - cc) if reverse else (lambda cc: cc)

    @pl.when(step == 0)
    def _():
        _page_fetch(cache_hbm, layer, pt_ref, slab, sem, 0, 0, order(0), pps, n_pages, True, lanes)

    @pl.when(step + 1 < nseq * nch)
    def _():
        nxt = step + 1
        _page_fetch(cache_hbm, layer, pt_ref, slab, sem, 1 - slot, nxt // nch, order(nxt % nch), pps, n_pages, True, lanes)

    _page_fetch(cache_hbm, layer, pt_ref, slab, sem, slot, b, order(c), pps, n_pages, False, lanes)
    return slot


def _mla_sample_kernel(pt_ref, q_ref, new_ref, wukt_ref, wuv_ref, cache_hbm, o_ref,
                       slab, sem, qa_s, qr_s, m_s, l_s, acc_s, kr_s, *, layer, pps, n_pages, nch, nseq, s_new):
    c = pl.program_id(1)
    slot = _paged_pipeline(cache_hbm, layer, pt_ref, slab, sem, pps, n_pages, nch, nseq, False)
    scale = (MLA_NOPE + MLA_ROPE) ** -0.5
    rows = MLA_H * s_new

    @pl.when(c == 0)
    def _():
        q = q_ref[0]
        pick = (lax.broadcasted_iota(I32, (LANES, LANES), 0) == lax.broadcasted_iota(I32, (LANES, LANES), 1) + 64).astype(BF)
        qa, qr = [], []
        for h in range(MLA_H):
            qh = q[:, h * LANES:(h + 1) * LANES]
            qa.append(jnp.dot(qh, wukt_ref[h], preferred_element_type=F32))
            qr.append(jnp.dot(qh, pick, preferred_element_type=F32))
        qa_s[...] = jnp.concatenate(qa, axis=0)
        qr_s[...] = jnp.concatenate(qr, axis=0)
        m_s[...] = jnp.full(m_s.shape, NEG, F32)
        l_s[...] = jnp.zeros_like(l_s)
        acc_s[...] = jnp.zeros_like(acc_s)
        kr_s[...] = jnp.zeros_like(kr_s)

    qa = qa_s[...]
    qr = qr_s[...]

    def attend(ckv_t, kr_t, mask):
        s = (_bdot(qa, ckv_t) + _bdot(qr, kr_t)) * scale
        if mask is not None:
            s = jnp.where(mask, s, NEG)
        m, l, acc = _softmax_step(s, (m_s[...], l_s[...], acc_s[...]), ckv_t, vt=True)
        m_s[...] = m
        l_s[...] = l
        acc_s[...] = acc

    nk = pps * PAGE
    kr_s[0:MLA_ROPE, 0:nk] = slab[slot, MLA_KVL:MLA_ROW, :]
    attend(slab[slot, 0:MLA_KVL, :], kr_s[:, 0:nk], None)

    @pl.when(c == nch - 1)
    def _():
        kr_s[0:MLA_ROPE, 0:PAGE] = new_ref[0, MLA_KVL:MLA_ROW, :]
        qi = lax.broadcasted_iota(I32, (rows, PAGE), 0) % s_new
        attend(new_ref[0, 0:MLA_KVL, :], kr_s[:, 0:PAGE], lax.broadcasted_iota(I32, (rows, PAGE), 1) <= qi)
        o_lat = acc_s[...] / l_s[...]
        o = None
        for h in range(MLA_H):
            part = _bdot(o_lat[h * s_new:(h + 1) * s_new], wuv_ref[h])
            o = part if o is None else o + part
        o_ref[0] = o


def _mla_sample(q, new_rows, cache, pt_flat, wl, layer, nseq, s_new, n_pages, pps):
    nch = n_pages // pps
    rows = MLA_H * s_new
    full = lambda a: pl.BlockSpec(a.shape, lambda b, c, pt: (0,) * a.ndim)
    return pl.pallas_call(
        functools.partial(_mla_sample_kernel, layer=layer, pps=pps, n_pages=n_pages, nch=nch, nseq=nseq, s_new=s_new),
        grid_spec=pltpu.PrefetchScalarGridSpec(
            num_scalar_prefetch=1,
            grid=(nseq, nch),
            in_specs=[pl.BlockSpec((1, s_new, 768), lambda b, c, pt: (b, 0, 0)),
                      pl.BlockSpec((1, MLA_ROW, PAGE), lambda b, c, pt: (b, 0, 0)),
                      full(wl["wukt"]), full(wl["wuv_h"]), pl.BlockSpec(memory_space=pl.ANY)],
            out_specs=pl.BlockSpec((1, s_new, 384), lambda b, c, pt: (b, 0, 0)),
            scratch_shapes=[pltpu.VMEM((2, MLA_ROW, pps * PAGE), F32), pltpu.SemaphoreType.DMA((2,)),
                            pltpu.VMEM((rows, MLA_KVL), F32), pltpu.VMEM((rows, LANES), F32),
                            pltpu.VMEM((rows, 1), F32), pltpu.VMEM((rows, 1), F32), pltpu.VMEM((rows, MLA_KVL), F32),
                            pltpu.VMEM((LANES, pps * PAGE), F32)],
        ),
        out_shape=jax.ShapeDtypeStruct((nseq, s_new, 384), F32),
        compiler_params=_cp(("arbitrary", "arbitrary")),
        name="mla_sample",
    )(pt_flat, q, new_rows, wl["wukt"], wl["wuv_h"], cache)


def _rev_cumsum_excl(x):
    lane = _lane(x.shape)
    inc = x
    sh = 1
    while sh < LANES:
        inc = inc + jnp.where(lane + sh < LANES, pltpu.roll(inc, LANES - sh, axis=1), 0.0)
        sh *= 2
    return inc - x


def _sb_sample_kernel(pt_ref, q_ref, new_ref, cache_hbm, o_ref, slab, sem, run_s, acc_s,
                      *, layer, pps, n_pages, nch, nseq, s_new):
    c = pl.program_id(1)
    slot = _paged_pipeline(cache_hbm, layer, pt_ref, slab, sem, pps, n_pages, nch, nseq, True)
    rows = SB_H * s_new
    q = q_ref[0].astype(F32) * HD ** -0.5
    own = lax.broadcasted_iota(I32, (rows, 256), 0) // s_new == lax.broadcasted_iota(I32, (rows, 256), 1) // HD
    q4 = jnp.where(own, _tile_r(q, SB_H), 0.0)

    @pl.when(c == 0)
    def _():
        z = _bdot(q4, new_ref[0, 0:256, :])
        l1mb, lsz = _sb_terms(z)
        qi = lax.broadcasted_iota(I32, (rows, PAGE), 0) % s_new
        strict = lax.broadcasted_iota(I32, (rows, PAGE), 1) < qi
        l1mb = jnp.where(strict, l1mb, 0.0)
        a = jnp.where(strict, jnp.exp(lsz + _rev_cumsum_excl(l1mb)), 0.0)
        acc_s[...] = _bdot_nt(a, new_ref[0, 256:512, :])
        run_s[...] = jnp.sum(l1mb, axis=1, keepdims=True)

    z = _bdot(q4, slab[slot, 0:256, :])
    l1mb, lsz = _sb_terms(z)
    run = run_s[...]
    pieces = [None] * pps
    for p in range(pps - 1, -1, -1):
        blk = l1mb[:, p * PAGE:(p + 1) * PAGE]
        pieces[p] = _rev_cumsum_excl(blk) + run
        run = run + jnp.sum(blk, axis=1, keepdims=True)
    run_s[...] = run
    a = jnp.exp(lsz + jnp.concatenate(pieces, axis=1))
    acc_s[...] += _bdot_nt(a, slab[slot, 256:512, :])

    @pl.when(c == nch - 1)
    def _():
        res = jnp.where(own, acc_s[...], 0.0)
        o = res[0:s_new]
        for h in range(1, SB_H):
            o = o + res[h * s_new:(h + 1) * s_new]
        o_ref[0] = o


def _sb_sample(q, new_rows, cache, pt_flat, layer, nseq, s_new, n_pages, pps):
    nch = n_pages // pps
    rows = SB_H * s_new
    return pl.pallas_call(
        functools.partial(_sb_sample_kernel, layer=layer, pps=pps, n_pages=n_pages, nch=nch, nseq=nseq, s_new=s_new),
        grid_spec=pltpu.PrefetchScalarGridSpec(
            num_scalar_prefetch=1,
            grid=(nseq, nch),
            in_specs=[pl.BlockSpec((1, s_new, 256), lambda b, c, pt: (b, 0, 0)),
                      pl.BlockSpec((1, 512, PAGE), lambda b, c, pt: (b, 0, 0)),
                      pl.BlockSpec(memory_space=pl.ANY)],
            out_specs=pl.BlockSpec((1, s_new, 256), lambda b, c, pt: (b, 0, 0)),
            scratch_shapes=[pltpu.VMEM((2, 512, pps * PAGE), F32), pltpu.SemaphoreType.DMA((2,)),
                            pltpu.VMEM((rows, 1), F32), pltpu.VMEM((rows, 256), F32)],
        ),
        out_shape=jax.ShapeDtypeStruct((nseq, s_new, 256), F32),
        compiler_params=_cp(("arbitrary", "arbitrary")),
        name="sb_sample",
    )(pt_flat, q, new_rows, cache)


def _cmp_sample_kernel(pt_ref, q_ref, cnew_ref, pe_ref, w1_ref, w2_ref, cache_hbm, o_ref, sel_ref, slab, sem,
                       *, layer, n_pages, nseq, s_new, past_len, nb_true, ns_true):
    slot = _paged_pipeline(cache_hbm, layer, pt_ref, slab, sem, n_pages, n_pages, 1, nseq, False, lanes=False)
    nb_past = past_len // CMP_BLOCK
    cb_past = _compress_rows(lambda j: slab[slot, pl.ds(j, nb_past, stride=CMP_BLOCK), :], pe_ref, w1_ref, w2_ref)
    cb = jnp.concatenate([cb_past, cnew_ref[0]], axis=0)
    qpos = past_len + lax.broadcasted_iota(I32, (s_new, 1), 0)
    o, sel = _cmp_select(q_ref[0], cb, qpos, s_new, nb_true, ns_true)
    o_ref[0] = o
    sel_ref[0] = sel


def _cmp_sample(nq_raw, cnew, cache, pt_flat, wl, layer, nseq, s_new, n_pages, nb_true, ns_true):
    past_len = n_pages * PAGE
    nbp = past_len // CMP_BLOCK + cnew.shape[1]
    assert nbp % LANES == 0
    nsp = _round_up(nbp // RATIO, LANES)
    full = lambda a: pl.BlockSpec(a.shape, lambda b, c, pt: (0,) * a.ndim)
    return pl.pallas_call(
        functools.partial(_cmp_sample_kernel, layer=layer, n_pages=n_pages, nseq=nseq, s_new=s_new,
                          past_len=past_len, nb_true=nb_true, ns_true=ns_true),
        grid_spec=pltpu.PrefetchScalarGridSpec(
            num_scalar_prefetch=1,
            grid=(nseq, 1),
            in_specs=[pl.BlockSpec((1, s_new, 384), lambda b, c, pt: (b, 0, 0)),
                      pl.BlockSpec((1, cnew.shape[1], LANES), lambda b, c, pt: (b, 0, 0)),
                      full(wl["pe"]), full(wl["w1"]), full(wl["w2"]), pl.BlockSpec(memory_space=pl.ANY)],
            out_specs=[pl.BlockSpec((1, s_new, 384), lambda b, c, pt: (b, 0, 0)),
                       pl.BlockSpec((1, s_new, nsp), lambda b, c, pt: (b, 0, 0))],
            scratch_shapes=[pltpu.VMEM((2, past_len, LANES), F32), pltpu.SemaphoreType.DMA((2,))],
        ),
        out_shape=[jax.ShapeDtypeStruct((nseq, s_new, 384), F32), jax.ShapeDtypeStruct((nseq, s_new, nsp), F32)],
        compiler_params=_cp(("arbitrary", "arbitrary")),
        name="nsa_cmp_sample",
    )(pt_flat, nq_raw, cnew, wl["pe"], wl["w1"], wl["w2"], cache)


def _nsa_sample_kernel(pt_ref, q_ref, sel_ref, slcn_ref, winn_ref, winbuf_ref, g_ref, ocmp_ref, cache_hbm, o_ref,
                       slab, sem, *, layer, n_pages, nseq, s_new, past_len):
    slot = _paged_pipeline(cache_hbm, layer, pt_ref, slab, sem, n_pages, n_pages, 1, nseq, False)
    rows = NSA_H * s_new
    qs = _stack_heads(q_ref[0], s_new) * HD ** -0.5
    sel = sel_ref[0]
    nsp = sel.shape[1]
    nsel_past = past_len // SLC_BLOCK
    qi = lax.broadcasted_iota(I32, (s_new, PAGE), 0)
    ki = lax.broadcasted_iota(I32, (s_new, PAGE), 1)
    init = (jnp.full((rows, 1), NEG, F32), jnp.zeros((rows, 1), F32), jnp.zeros((rows, LANES), F32))
    kv_t = slab[slot].astype(BF)
    expand = (lax.broadcasted_iota(I32, (nsp, past_len), 1) // SLC_BLOCK == lax.broadcasted_iota(I32, (nsp, past_len), 0)).astype(BF)
    chosen = jnp.dot(sel.astype(BF), expand, preferred_element_type=F32) > 0.5
    s = jnp.where(_tile_mask(chosen, NSA_H), _bdot(qs, kv_t), NEG)
    carry = _softmax_step(s, init, kv_t, vt=True)
    kvn = slcn_ref[0].astype(BF)
    new_sel = jnp.sum(jnp.where(_lane(sel.shape) == nsel_past, sel, 0.0), axis=1, keepdims=True) > 0.5
    ok = jnp.logical_and(new_sel, ki <= qi)
    s = jnp.where(_tile_mask(ok, NSA_H), _bdot_nt(qs, kvn), NEG)
    _, l, acc = _softmax_step(s, carry, kvn)
    o_slc = _unstack_heads(acc / l, s_new)
    wb = winbuf_ref[0].astype(BF)
    wr = wb.shape[0]
    jpos = lax.broadcasted_iota(I32, (s_new, wr), 1)
    qrow = lax.broadcasted_iota(I32, (s_new, wr), 0)
    ok = (jpos - wr) > (qrow - WINDOW)
    s = jnp.where(_tile_mask(ok, NSA_H), _bdot_nt(qs, wb), NEG)
    carry = _softmax_step(s, init, wb)
    wn = winn_ref[0].astype(BF)
    s = jnp.where(_tile_mask(ki <= qi, NSA_H), _bdot_nt(qs, wn), NEG)
    _, l, acc = _softmax_step(s, carry, wn)
    o_win = _unstack_heads(acc / l, s_new)
    o_ref[0] = _gate_merge(g_ref[0], ocmp_ref[0], o_slc, o_win, s_new)


def _nsa_sample(nq_rot, sel, slc_new, win_new, win_buf, gates, o_cmp, cache, pt_flat, layer, nseq, s_new, n_pages):
    past_len = n_pages * PAGE
    per_seq = lambda a: pl.BlockSpec((1,) + a.shape[1:], lambda b, c, pt: (b,) + (0,) * (a.ndim - 1))
    return pl.pallas_call(
        functools.partial(_nsa_sample_kernel, layer=layer, n_pages=n_pages, nseq=nseq, s_new=s_new, past_len=past_len),
        grid_spec=pltpu.PrefetchScalarGridSpec(
            num_scalar_prefetch=1,
            grid=(nseq, 1),
            in_specs=[per_seq(nq_rot), per_seq(sel), per_seq(slc_new), per_seq(win_new), per_seq(win_buf),
                      per_seq(gates), per_seq(o_cmp), pl.BlockSpec(memory_space=pl.ANY)],
            out_specs=pl.BlockSpec((1, s_new, 384), lambda b, c, pt: (b, 0, 0)),
            scratch_shapes=[pltpu.VMEM((2, LANES, past_len), F32), pltpu.SemaphoreType.DMA((2,))],
        ),
        out_shape=jax.ShapeDtypeStruct((nseq, s_new, 384), F32),
        compiler_params=_cp(("arbitrary", "arbitrary")),
        name="nsa_sample",
    )(pt_flat, nq_rot, sel, slc_new, win_new, win_buf, gates, o_cmp, cache)


def _rot_half(w, half):
    return jnp.concatenate([-w[..., half:], w[..., :half]], axis=-1)


def _rope_tables(pos):
    pos = pos.astype(F32)[:, None]
    t = pos.shape[0]

    def cs(half):
        inv = 1.0 / (ROPE_THETA ** (jnp.arange(half, dtype=F32) / half))
        ang = pos * inv[None, :]
        return jnp.cos(ang), jnp.sin(ang)

    c16, s16 = cs(MLA_ROPE // 2)
    c32, s32 = cs(HD // 2)
    one = lambda w: jnp.ones((t, w), F32)
    zero = lambda w: jnp.zeros((t, w), F32)
    cq = jnp.concatenate([one(64), c16, c16, zero(32)], 1)
    sq = jnp.concatenate([zero(64), s16, s16, zero(32)], 1)
    ckr = jnp.concatenate([c16, c16, zero(32), c16, c16, zero(32)], 1)
    skr = jnp.concatenate([s16, s16, zero(32), s16, s16, zero(32)], 1)
    cn = jnp.concatenate([c32, c32, c32, c32], 1)
    sn = jnp.concatenate([s32, s32, s32, s32], 1)
    c2 = jnp.concatenate([c32, c32, one(64)], 1)
    s2 = jnp.concatenate([s32, s32, zero(64)], 1)
    return jnp.concatenate([cq, sq, ckr, skr, cn, sn, c2, s2], 1)


def _prep_layer(l, w_in, mla_q_norm, mla_w_uq, mla_kv_norm, mla_w_uk, mla_w_uv, nsa_cmp_pe, nsa_cmp_w1, nsa_cmp_w2,
                grp_norm, w_out, ln1_g, ln1_b, w_router, router_bias, moe_w_gate, moe_w_up, moe_w_down,
                sh_w_gate, sh_w_up, sh_w_down, ln2_g, ln2_b):
    w = w_in[l]
    d = w.shape[0]
    z = lambda n: jnp.zeros((d, n), F32)
    kr = w[:, 640:672]
    krr = _rot_half(kr, MLA_ROPE // 2)
    nq = w[:, 672:1056]
    nqp = _rot_half(nq.reshape(d, NSA_H, HD), HD // 2).reshape(d, NSA_H * HD)
    nkv = w[:, 1056:1440]
    slcp = jnp.concatenate([_rot_half(w[:, 1184:1248], HD // 2), z(64)], 1)
    winp = jnp.concatenate([_rot_half(w[:, 1312:1376], HD // 2), z(64)], 1)
    gates = jnp.concatenate([w[:, 1440:1458], z(LANES - 18)], 1)
    w_ext = jnp.concatenate([w[:, 0:384], w[:, 384:640], kr, z(32), kr, z(32), krr, z(32), krr, z(32),
                             nq, nqp, nkv, slcp, winp, gates, w[:, 1458:2226]], 1).astype(BF)
    assert w_ext.shape[1] == W_EXT
    uq = mla_w_uq[l]
    r = uq.shape[0]
    zq = lambda n: jnp.zeros((r, MLA_H, n), F32)
    slab1 = jnp.concatenate([uq, zq(32)], -1).reshape(r, MLA_H * LANES)
    slab2 = jnp.concatenate([zq(64), _rot_half(uq[..., MLA_NOPE:], MLA_ROPE // 2), zq(32)], -1).reshape(r, MLA_H * LANES)
    uk = mla_w_uk[l]
    uv = mla_w_uv[l]
    wuk = jnp.concatenate([uk, jnp.zeros_like(uk)], -1).reshape(MLA_KVL, MLA_H * LANES)
    ukt = jnp.transpose(uk, (1, 2, 0))
    wukt = jnp.concatenate([ukt, jnp.zeros_like(ukt)], 1)
    eye = jnp.eye(MLA_H, dtype=F32)
    wuv_h = jnp.einsum("rhv,hg->hrgv", uv, eye).reshape(MLA_H, MLA_KVL, MLA_H * HD)
    w1 = nsa_cmp_w1[l][:, 0]
    w1 = w1.reshape(2, CMP_BLOCK, HD, HD)
    zz = jnp.zeros((CMP_BLOCK, HD, HD), F32)
    w1bd = jnp.concatenate([jnp.concatenate([w1[0], zz], -1), jnp.concatenate([zz, w1[1]], -1)], 1)
    w1bd = w1bd.reshape(CMP_BLOCK // 2, 2 * LANES, LANES)
    w2 = nsa_cmp_w2[l][:, 0]
    z2 = jnp.zeros((HD, HD), F32)
    w2bd = jnp.concatenate([jnp.concatenate([w2[0], z2], -1), jnp.concatenate([z2, w2[1]], -1)], 0)
    return {
        "w_ext": w_ext, "q_norm": mla_q_norm[l][None], "wuq": jnp.concatenate([slab1, slab2], 1).astype(BF),
        "kv_norm": mla_kv_norm[l][None], "wuk": wuk.astype(BF), "wuv": uv.reshape(MLA_KVL, MLA_H * HD).astype(BF),
        "wukt": wukt.astype(BF), "wuv_h": wuv_h.astype(BF),
        "pe": nsa_cmp_pe[l].reshape(CMP_BLOCK, LANES), "w1": w1bd.astype(BF), "w2": w2bd.astype(BF),
        "grp_norm": grp_norm[l][None], "w_out": w_out[l].astype(BF), "ln1_g": ln1_g[l][None], "ln1_b": ln1_b[l][None],
        "w_router": w_router[l].astype(BF), "router_bias": router_bias[l][None],
        "moe_wg": moe_w_gate[l].astype(BF), "moe_wu": moe_w_up[l].astype(BF), "moe_wd": moe_w_down[l].astype(BF),
        "sh_g": sh_w_gate[l].astype(BF), "sh_u": sh_w_up[l].astype(BF), "sh_d": sh_w_down[l].astype(BF),
        "ln2_g": ln2_g[l][None], "ln2_b": ln2_b[l][None],
    }


def _pick(n, pref):
    t = min(pref, n)
    while n % t:
        t //= 2
    return t


def _prompt_layer(hp, wl, tab, b, t):
    n = b * t
    tm = _pick(t, 512)
    (rows, q, k, v, nqr, nqt, cmp_r, slc_r, win_r, slc_b, win_b, gates, sbq, sb_r, sb_b) = _proj(hp, wl, tab, tm, t // tm)
    tq = _pick(t, 256)
    o_mla = _mla_attn(q, k, v, b, t, _pick(t, MLA_TQ), _pick(t, MLA_TK))
    o_sb = _sb_attn(sbq, sb_b, b, t, tq)
    nblk = n // CMP_BLOCK
    cblk = _compress(cmp_r.reshape(nblk, CMP_BLOCK * LANES), wl, _pick(nblk, 256))
    tqn = _pick(t, 128)
    o_cmp, sel = _cmp_prompt(nqr, cblk.reshape(b, t // CMP_BLOCK, LANES), b, t, tqn)
    o_nsa = _nsa_attn(nqt, slc_b, win_b, sel, gates, o_cmp, b, t, tqn, _pick(t, NSA_TK))
    h1 = _mixout(hp, o_mla, o_nsa, o_sb, wl, _pick(n, 512))
    h2 = _moe(h1, wl)
    return h2, rows, cmp_r, slc_r, win_r, sb_r


def _sample_layer(hs, wl, tab, l, caches, pt_flat, win_buf, db, s, n_pages):
    cache_mla, cache_cmp, cache_slc, cache_sb = caches
    n = db * s
    past_len = n_pages * PAGE
    tm = _pick(n, 512)
    (rows, q, k, v, nqr, nqt, cmp_r, slc_r, win_r, slc_b, win_b, gates, sbq, sb_r, sb_b) = _proj(hs, wl, tab, tm, 1)
    seq = lambda a: a.reshape(db, s, a.shape[-1])
    keys = lambda a: jnp.pad(seq(a), ((0, 0), (0, PAGE - s), (0, 0)))
    keys_t = lambda a: jnp.swapaxes(keys(a), 1, 2)
    o_mla = _mla_sample(seq(q), keys_t(rows), cache_mla, pt_flat, wl, l, db, s, n_pages, _pick(n_pages, 32))
    o_sb = _sb_sample(seq(sbq), keys_t(sb_r), cache_sb, pt_flat, l, db, s, n_pages, _pick(n_pages, 16))
    sp = -(-s // SLC_BLOCK) * SLC_BLOCK
    cmp_pad = jnp.pad(seq(cmp_r), ((0, 0), (0, sp - s), (0, 0)))
    nb_new = sp // CMP_BLOCK
    cnew = _compress(cmp_pad.reshape(db * nb_new, CMP_BLOCK * LANES), wl, _pick(db * nb_new, 256)).reshape(db, nb_new, LANES)
    nb_true = past_len // CMP_BLOCK + nb_new
    ns_true = nb_true // RATIO
    cnew = jnp.pad(cnew, ((0, 0), (0, (-nb_true) % LANES), (0, 0)))
    o_cmp, sel = _cmp_sample(seq(nqr), cnew, cache_cmp, pt_flat, wl, l, db, s, n_pages, nb_true, ns_true)
    o_nsa = _nsa_sample(seq(nqt), sel, keys(slc_r), keys(win_r), win_buf, seq(gates), o_cmp,
                        cache_slc, pt_flat, l, db, s, n_pages)
    h1 = _mixout(hs, o_mla.reshape(n, 384), o_nsa.reshape(n, 384), o_sb.reshape(n, 256), wl, _pick(n, 512))
    h2 = _moe(h1, wl)
    return h2, rows, cmp_r, slc_r, win_r, sb_r


def kernel(x_prompt, x_sample, cache_mla, cache_nsa_cmp, cache_nsa_slc, state_nsa_win, cache_sb, page_table, w_in, mla_q_norm, mla_w_uq, mla_kv_norm, mla_w_uk, mla_w_uv, nsa_cmp_pe, nsa_cmp_w1, nsa_cmp_w2, grp_norm, w_out, ln1_g, ln1_b, w_router, router_bias, moe_w_gate, moe_w_up, moe_w_down, sh_w_gate, sh_w_up, sh_w_down, ln2_g, ln2_b):
    b, t, d = x_prompt.shape
    db, s, _ = x_sample.shape
    n_pages = page_table.shape[1]
    past_len = n_pages * PAGE
    depth = w_in.shape[0]
    weights = (w_in, mla_q_norm, mla_w_uq, mla_kv_norm, mla_w_uk, mla_w_uv, nsa_cmp_pe, nsa_cmp_w1, nsa_cmp_w2,
               grp_norm, w_out, ln1_g, ln1_b, w_router, router_bias, moe_w_gate, moe_w_up, moe_w_down,
               sh_w_gate, sh_w_up, sh_w_down, ln2_g, ln2_b)
    tab_p = _rope_tables(jnp.arange(t))
    tm_s = _pick(db * s, 512)
    tab_s = jnp.tile(_rope_tables(past_len + jnp.arange(s)), (tm_s // s, 1))
    pt_flat = page_table.reshape(-1).astype(I32)
    rows2d = lambda c: c.reshape(c.shape[:3] + (-1,))
    pages_t = lambda c: jnp.swapaxes(rows2d(c), 2, 3)
    caches = (pages_t(cache_mla), rows2d(cache_nsa_cmp), pages_t(cache_nsa_slc), pages_t(cache_sb))
    hp = x_prompt.reshape(b * t, d)
    hs = x_sample.reshape(db * s, d)
    outs = [[] for _ in range(10)]
    wr = state_nsa_win.shape[2]
    keep = min(WINDOW, wr + s)
    for l in range(depth):
        wl = _prep_layer(l, *weights)
        hp, rows, cmp_r, slc_r, win_r, sb_r = _prompt_layer(hp, wl, tab_p, b, t)
        outs[0].append(rows.reshape(b, t, MLA_ROW))
        outs[2].append(cmp_r.reshape(b, t, 2, 1, HD))
        outs[4].append(slc_r.reshape(b, t, 2, 1, HD))
        outs[6].append(win_r.reshape(b, t, 2, 1, HD)[:, t - min(WINDOW, t):])
        outs[8].append(sb_r.reshape(b, t, 2, SB_H, HD))
        win_buf = state_nsa_win[l].reshape(db, wr, LANES)
        hs, rows, cmp_r, slc_r, win_r, sb_r = _sample_layer(hs, wl, tab_s, l, caches, pt_flat, win_buf, db, s, n_pages)
        outs[1].append(rows.reshape(db, s, MLA_ROW))
        outs[3].append(cmp_r.reshape(db, s, 2, 1, HD))
        outs[5].append(slc_r.reshape(db, s, 2, 1, HD))
        new_win = jnp.concatenate([win_buf, win_r.reshape(db, s, LANES)], axis=1)[:, wr + s - keep:]
        outs[7].append(new_win.reshape(db, keep, 2, 1, HD))
        outs[9].append(sb_r.reshape(db, s, 2, SB_H, HD))
    st = [jnp.stack(o) for o in outs]
    return (hp.reshape(b, t, d), hs.reshape(db, s, d), st[0], st[1], st[2], st[3], st[4], st[5], st[6], st[7], st[8], st[9])
```
